```python
import math
import jax, jax.numpy as jnp
from jax import lax
import numpy as np

D_MODEL = 2048
BATCH = 4
SEQ = 8192
DEPTH = 4
DEC_BATCH = 16
DEC_SEQ = 16
PAST_LEN = 2048

CHUNK = 64
N_HEADS = 16
HEAD_DIM = 64
ATT_W = N_HEADS * HEAD_DIM
S5_GROUP = 16
S5_W = D_MODEL // 2
S5_GROUPS = S5_W // S5_GROUP
S5_STATE = 64
D_FF = 5632
CONV_W = 3
Q_BLOCK = 128
RMS_EPS = 1e-6
SPLITS = [ATT_W, 2 * ATT_W, 3 * ATT_W, 3 * ATT_W + N_HEADS, 3 * ATT_W + N_HEADS + S5_W]
N_IN = 3 * ATT_W + N_HEADS + S5_W + 2 * D_MODEL

kernel_name = 'fox_s5_convffn_streaming_step'


def rmsnorm(x, g):
    xf = x.astype(jnp.float32)
    inv = lax.rsqrt(jnp.mean(xf * xf, axis=-1, keepdims=True) + RMS_EPS)
    return (xf * inv).astype(x.dtype) * g


def fox_attention(q, k, v, cq, ck, q_pos, k_pos):
    bsz, sq = q.shape[0], q.shape[1]
    blk = min(Q_BLOCK, sq)
    nb = sq // blk
    scale = HEAD_DIM ** -0.5
    qb = q.reshape(bsz, nb, blk, N_HEADS, HEAD_DIM).transpose(1, 0, 2, 3, 4)
    cqb = cq.reshape(bsz, nb, blk, N_HEADS).transpose(1, 0, 2, 3)
    pb = q_pos.reshape(nb, blk)
    ck_t = ck.transpose(0, 2, 1)

    def one_block(args):
        qi, ci, pi = args
        s = jnp.einsum('bqhd,bkhd->bhqk', qi, k, preferred_element_type=jnp.float32) * scale
        bias = ci.transpose(0, 2, 1)[..., :, None] - ck_t[..., None, :]
        mask = k_pos[None, :] <= pi[:, None]
        p = jax.nn.softmax(jnp.where(mask, s + bias, -jnp.inf), axis=-1)
        return jnp.einsum('bhqk,bkhd->bqhd', p.astype(v.dtype), v)

    o = lax.map(one_block, (qb, cqb, pb))
    return o.transpose(1, 0, 2, 3, 4).reshape(bsz, sq, ATT_W)


def s5_mixer(u, lam_re, lam_im, log_dt, b_re, b_im, c_re, c_im, d_skip, h0_re, h0_im):
    f32 = jnp.float32
    bsz, L = u.shape[0], u.shape[1]
    uf = u.astype(f32).reshape(bsz, L, S5_GROUPS, S5_GROUP)
    lr, li = lam_re.astype(f32), lam_im.astype(f32)
    dt = jnp.exp(log_dt.astype(f32))[:, None]
    mag = jnp.exp(lr * dt)
    ab_re, ab_im = mag * jnp.cos(li * dt), mag * jnp.sin(li * dt)
    den = lr * lr + li * li
    nr = ab_re - 1.0
    f_re = (nr * lr + ab_im * li) / den
    f_im = (ab_im * lr - nr * li) / den
    br, bi = b_re.astype(f32), b_im.astype(f32)
    bb_re = f_re[..., None] * br - f_im[..., None] * bi
    bb_im = f_re[..., None] * bi + f_im[..., None] * br
    x_re = jnp.einsum('blgh,gph->blgp', uf, bb_re)
    x_im = jnp.einsum('blgh,gph->blgp', uf, bb_im)
    a_re = jnp.broadcast_to(ab_re[None, None], (1, L, S5_GROUPS, S5_STATE))
    a_im = jnp.broadcast_to(ab_im[None, None], (1, L, S5_GROUPS, S5_STATE))

    def combine(e1, e2):
        a1r, a1i, b1r, b1i = e1
        a2r, a2i, b2r, b2i = e2
        return (a2r * a1r - a2i * a1i, a2r * a1i + a2i * a1r,
                a2r * b1r - a2i * b1i + b2r, a2r * b1i + a2i * b1r + b2i)

    pa_re, pa_im, s_re, s_im = lax.associative_scan(combine, (a_re, a_im, x_re, x_im), axis=1)
    if h0_re is not None:
        hr = h0_re.astype(f32)[:, None]
        hi = h0_im.astype(f32)[:, None]
        s_re, s_im = (s_re + pa_re * hr - pa_im * hi, s_im + pa_re * hi + pa_im * hr)
    y = (jnp.einsum('blgp,ghp->blgh', s_re, c_re.astype(f32))
         - jnp.einsum('blgp,ghp->blgh', s_im, c_im.astype(f32))
         + d_skip.astype(f32).reshape(S5_GROUPS, S5_GROUP) * uf)
    return y.reshape(bsz, L, S5_W).astype(u.dtype), s_re[:, -1], s_im[:, -1]


def causal_dwconv(a, w, buf):
    L = a.shape[1]
    full = jnp.concatenate([buf, a], axis=1)
    out = full[:, 0:L] * w[0]
    for i in range(1, CONV_W):
        out = out + full[:, i:i + L] * w[i]
    return out, full[:, -(CONV_W - 1):]


def block(x, w, cache):
    f32 = jnp.float32
    bsz, L = x.shape[0], x.shape[1]
    h = rmsnorm(x, w['g_mix_pre'])
    z = h @ w['w_in']
    q, k, v, fl, u, gl = jnp.split(z, SPLITS, axis=-1)
    q = q.reshape(bsz, L, N_HEADS, HEAD_DIM)
    k = k.reshape(bsz, L, N_HEADS, HEAD_DIM)
    v = v.reshape(bsz, L, N_HEADS, HEAD_DIM)
    logf = jax.nn.log_sigmoid((fl + w['b_f']).astype(f32))
    if cache is None:
        past = 0
        k_all, v_all, logf_all = k, v, logf
        h0_re, h0_im = None, None
        conv_buf = jnp.zeros((bsz, CONV_W - 1, D_FF), x.dtype)
    else:
        past = cache['k'].shape[1]
        k_all = jnp.concatenate([cache['k'], k], axis=1)
        v_all = jnp.concatenate([cache['v'], v], axis=1)
        logf_all = jnp.concatenate([cache['logf'].astype(f32), logf], axis=1)
        h0_re, h0_im = cache['ssm_re'], cache['ssm_im']
        conv_buf = cache['conv']
    c_all = jnp.cumsum(logf_all, axis=1)
    k_pos = jnp.arange(past + L)
    q_pos = past + jnp.arange(L)
    o_att = fox_attention(q, k_all, v_all, c_all[:, past:], c_all, q_pos, k_pos)
    y_a = o_att @ w['w_att_proj']
    y_s5, s_re, s_im = s5_mixer(u, w['lam_re'], w['lam_im'], w['log_dt'], w['b_re'], w['b_im'],
                                w['c_re'], w['c_im'], w['d_skip'], h0_re, h0_im)
    g_s5 = jax.nn.gelu(y_s5)
    y_b = (g_s5 @ w['w_glu_v']) * jax.nn.sigmoid(g_s5 @ w['w_glu_g'])
    g_a, g_b = jnp.split(jax.nn.sigmoid(gl), 2, axis=-1)
    mix = (g_a * y_a + g_b * y_b) @ w['w_out']
    x = x + rmsnorm(mix, w['g_mix_post'])
    h2 = rmsnorm(x, w['g_ffn_pre'])
    a_up, b_up = jnp.split(h2 @ w['w_up'], 2, axis=-1)
    a_c, new_buf = causal_dwconv(a_up, w['conv_w'], conv_buf)
    ffn = (jax.nn.gelu(a_c) * b_up) @ w['w_down']
    x = x + rmsnorm(ffn, w['g_ffn_post'])
    return x, (k, v, logf.astype(x.dtype), s_re.astype(x.dtype), s_im.astype(x.dtype), new_buf)


def trunk(x, weights, caches):
    states = []
    for l in range(DEPTH):
        wl = {name: arr[l] for name, arr in weights.items()}
        cl = None if caches is None else {name: arr[l] for name, arr in caches.items()}
        x, st = block(x, wl, cl)
        states.append(st)
    stacked = [jnp.stack([st[i] for st in states]) for i in range(6)]
    return x, stacked


def setup_inputs(seed: int = 0) -> dict:
    key = jax.random.key(seed)
    ks = jax.random.split(key, 32)
    f32 = jnp.float32

    def nrm(k, shape, scale):
        return jax.random.normal(k, shape, f32) * scale

    def gain(k):
        return 1.0 + 0.02 * jax.random.normal(k, (DEPTH, D_MODEL), f32)

    n_idx = jnp.arange(S5_STATE, dtype=f32)
    return {
        'x_prompt': nrm(ks[0], (BATCH, SEQ, D_MODEL), 1.0),
        'x_sample': nrm(ks[1], (DEC_BATCH, DEC_SEQ, D_MODEL), 1.0),
        'cache_k': nrm(ks[2], (DEPTH, DEC_BATCH, PAST_LEN, N_HEADS, HEAD_DIM), 1.0),
        'cache_v': nrm(ks[3], (DEPTH, DEC_BATCH, PAST_LEN, N_HEADS, HEAD_DIM), 1.0),
        'cache_logf': jax.nn.log_sigmoid(3.0 + nrm(ks[4], (DEPTH, DEC_BATCH, PAST_LEN, N_HEADS), 1.0)),
        'state_ssm_re': nrm(ks[5], (DEPTH, DEC_BATCH, S5_GROUPS, S5_STATE), 0.1),
        'state_ssm_im': nrm(ks[6], (DEPTH, DEC_BATCH, S5_GROUPS, S5_STATE), 0.1),
        'state_conv': nrm(ks[7], (DEPTH, DEC_BATCH, CONV_W - 1, D_FF), 1.0),
        'g_mix_pre': gain(ks[8]),
        'w_in': nrm(ks[9], (DEPTH, D_MODEL, N_IN), D_MODEL ** -0.5),
        'b_f': jax.random.uniform(ks[10], (DEPTH, N_HEADS), f32, 2.0, 5.0),
        'lam_re': -0.5 + nrm(ks[11], (DEPTH, S5_GROUPS, S5_STATE), 0.01),
        'lam_im': math.pi * n_idx + nrm(ks[12], (DEPTH, S5_GROUPS, S5_STATE), 0.01),
        'log_dt': jax.random.uniform(ks[13], (DEPTH, S5_GROUPS), f32, math.log(1e-3), math.log(1e-1)),
        'b_re': nrm(ks[14], (DEPTH, S5_GROUPS, S5_STATE, S5_GROUP), (2 * S5_GROUP) ** -0.5),
        'b_im': nrm(ks[15], (DEPTH, S5_GROUPS, S5_STATE, S5_GROUP), (2 * S5_GROUP) ** -0.5),
        'c_re': nrm(ks[16], (DEPTH, S5_GROUPS, S5_GROUP, S5_STATE), S5_STATE ** -0.5),
        'c_im': nrm(ks[17], (DEPTH, S5_GROUPS, S5_GROUP, S5_STATE), S5_STATE ** -0.5),
        'd_skip': nrm(ks[18], (DEPTH, S5_W), 1.0),
        'w_att_proj': nrm(ks[19], (DEPTH, ATT_W, D_MODEL), ATT_W ** -0.5),
        'w_glu_v': nrm(ks[20], (DEPTH, S5_W, D_MODEL), S5_W ** -0.5),
        'w_glu_g': nrm(ks[21], (DEPTH, S5_W, D_MODEL), S5_W ** -0.5),
        'w_out': nrm(ks[22], (DEPTH, D_MODEL, D_MODEL), D_MODEL ** -0.5),
        'g_mix_post': gain(ks[23]),
        'g_ffn_pre': gain(ks[24]),
        'w_up': nrm(ks[25], (DEPTH, D_MODEL, 2 * D_FF), D_MODEL ** -0.5),
        'conv_w': nrm(ks[26], (DEPTH, CONV_W, D_FF), CONV_W ** -0.5),
        'w_down': nrm(ks[27], (DEPTH, D_FF, D_MODEL), D_FF ** -0.5),
        'g_ffn_post': gain(ks[28]),
    }


def reference(x_prompt, x_sample, cache_k, cache_v, cache_logf, state_ssm_re, state_ssm_im, state_conv,
              g_mix_pre, w_in, b_f, lam_re, lam_im, log_dt, b_re, b_im, c_re, c_im, d_skip,
              w_att_proj, w_glu_v, w_glu_g, w_out, g_mix_post, g_ffn_pre, w_up, conv_w, w_down, g_ffn_post):
    weights = {
        'g_mix_pre': g_mix_pre, 'w_in': w_in, 'b_f': b_f, 'lam_re': lam_re, 'lam_im': lam_im,
        'log_dt': log_dt, 'b_re': b_re, 'b_im': b_im, 'c_re': c_re, 'c_im': c_im, 'd_skip': d_skip,
        'w_att_proj': w_att_proj, 'w_glu_v': w_glu_v, 'w_glu_g': w_glu_g, 'w_out': w_out,
        'g_mix_post': g_mix_post, 'g_ffn_pre': g_ffn_pre, 'w_up': w_up, 'conv_w': conv_w,
        'w_down': w_down, 'g_ffn_post': g_ffn_post,
    }
    caches = {'k': cache_k, 'v': cache_v, 'logf': cache_logf, 'ssm_re': state_ssm_re,
              'ssm_im': state_ssm_im, 'conv': state_conv}
    y_prompt, p_st = trunk(x_prompt, weights, None)
    y_sample, s_st = trunk(x_sample, weights, caches)
    p_k, p_v, p_logf, p_ssm_re, p_ssm_im, p_conv = p_st
    s_k, s_v, s_logf, s_ssm_re, s_ssm_im, s_conv = s_st
    return (y_prompt, y_sample, p_k, p_v, p_logf, p_ssm_re, p_ssm_im, p_conv,
            s_k, s_v, s_logf, s_ssm_re, s_ssm_im, s_conv)
```

```python
import functools
import math

import jax
import jax.numpy as jnp
from jax import lax
from jax.experimental import pallas as pl
from jax.experimental.pallas import tpu as pltpu

F32 = jnp.float32
BF16 = jnp.bfloat16
HIGHEST = lax.Precision.HIGHEST

RMS_EPS = 1e-6
N_HEADS = 16
HEAD_DIM = 64
ATT_W = N_HEADS * HEAD_DIM
S5_GROUP = 16
S5_STATE = 64
CONV_W = 3
LANES = 128
SUBLANES = 8
S5_CHUNK = 16
VMEM_LIMIT_BYTES = 56 * 1024 * 1024


def _params(*sem):
    return pltpu.CompilerParams(dimension_semantics=sem, vmem_limit_bytes=VMEM_LIMIT_BYTES)


def _gelu_tanh(x):
    return 0.5 * x * (1.0 + jnp.tanh(0.7978845608028654 * (x + 0.044715 * (x * x * x))))


def _sigmoid(x):
    return 1.0 / (1.0 + jnp.exp(-x))


def _log_sigmoid(x):
    return jnp.minimum(x, 0.0) - jnp.log1p(jnp.exp(-jnp.abs(x)))


def _rms_scale(x):
    return lax.rsqrt(jnp.mean(x * x, axis=-1, keepdims=True) + RMS_EPS)


def _norm_in_kernel(x_ref, g_ref, w_ref, wf_ref, bf_ref, z_ref, lf_ref, h_ref):
    @pl.when(pl.program_id(1) == 0)
    def _():
        x = x_ref[...]
        h = ((x * _rms_scale(x)) * g_ref[...]).astype(BF16)
        h_ref[...] = h
        fl = jnp.dot(h, wf_ref[...], preferred_element_type=F32) + bf_ref[...]
        lf_ref[...] = _log_sigmoid(fl)

    z_ref[...] = jnp.dot(h_ref[...], w_ref[...], preferred_element_type=F32)


def _norm_in(x, g, w, wf, bf, tm, tn):
    t, d = x.shape
    n = w.shape[1]
    return pl.pallas_call(
        _norm_in_kernel,
        grid=(t // tm, n // tn),
        in_specs=[
            pl.BlockSpec((tm, d), lambda i, j: (i, 0)),
            pl.BlockSpec((1, d), lambda i, j: (0, 0)),
            pl.BlockSpec((d, tn), lambda i, j: (0, j)),
            pl.BlockSpec((d, LANES), lambda i, j: (0, 0)),
            pl.BlockSpec((1, LANES), lambda i, j: (0, 0)),
        ],
        out_specs=[
            pl.BlockSpec((tm, tn), lambda i, j: (i, j)),
            pl.BlockSpec((tm, LANES), lambda i, j: (i, 0)),
        ],
        out_shape=[jax.ShapeDtypeStruct((t, n), F32), jax.ShapeDtypeStruct((t, LANES), F32)],
        scratch_shapes=[pltpu.VMEM((tm, d), BF16)],
        compiler_params=_params("arbitrary", "arbitrary"),
        name="norm_in",
    )(x, g, w, wf, bf)


def _cumsum_kernel(x_ref, o_ref, carry_ref):
    @pl.when(pl.program_id(1) == 0)
    def _():
        carry_ref[...] = jnp.zeros_like(carry_ref)

    x = x_ref[...]
    row = lax.broadcasted_iota(jnp.int32, (LANES, LANES), 0)
    col = lax.broadcasted_iota(jnp.int32, (LANES, LANES), 1)
    upper = (row <= col).astype(F32)
    ones = jnp.ones((LANES, LANES), F32)
    tn_dims = (((0,), (0,)), ((), ()))
    ct = lax.dot_general(x, upper, tn_dims, precision=HIGHEST, preferred_element_type=F32)
    tot = lax.dot_general(x, ones, tn_dims, precision=HIGHEST, preferred_element_type=F32)
    o_ref[...] = -(ct + carry_ref[...])
    carry_ref[...] += tot


def _neg_cumsum_t(x):
    b, l, _ = x.shape
    return pl.pallas_call(
        _cumsum_kernel,
        grid=(b, l // LANES),
        in_specs=[pl.BlockSpec((None, LANES, LANES), lambda i, j: (i, j, 0))],
        out_specs=pl.BlockSpec((None, LANES, LANES), lambda i, j: (i, 0, j)),
        out_shape=jax.ShapeDtypeStruct((b, LANES, l), F32),
        scratch_shapes=[pltpu.VMEM((LANES, LANES), F32)],
        compiler_params=_params("arbitrary", "arbitrary"),
        name="cumsum",
    )(x)


def _head_masks():
    lane = lax.broadcasted_iota(jnp.int32, (1, LANES), 1)
    return lane < HEAD_DIM, lane >= HEAD_DIM


def _attn_kernel(q_ref, k_ref, v_ref, nck_ref, o_ref, kb_ref, vb_ref, m_ref, l_ref, acc_ref, *, tq):
    i = pl.program_id(2)

    @pl.when(i == 0)
    def _():
        kb_ref[...] = k_ref[...].astype(BF16)
        vb_ref[...] = v_ref[...].astype(BF16)

    q = q_ref[...]
    masks = _head_masks()
    qh = [jnp.where(mk, q, 0.0).astype(BF16) for mk in masks]
    m_ref[...] = jnp.full_like(m_ref, -jnp.inf)
    l_ref[...] = jnp.zeros_like(l_ref)
    acc_ref[...] = jnp.zeros_like(acc_ref)
    nt_dims = (((1,), (1,)), ((), ()))

    def step(kb, diagonal):
        start = pl.multiple_of(kb * tq, tq)
        kblk = kb_ref[pl.ds(start, tq), :]
        vblk = vb_ref[pl.ds(start, tq), :]
        for h in range(2):
            s = lax.dot_general(qh[h], kblk, nt_dims, preferred_element_type=F32)
            s = s + nck_ref[h, pl.ds(kb, 1), :]
            if diagonal:
                r = lax.broadcasted_iota(jnp.int32, (tq, tq), 0)
                c = lax.broadcasted_iota(jnp.int32, (tq, tq), 1)
                s = jnp.where(c <= r, s, -jnp.inf)
            m_old = m_ref[h]
            m_new = jnp.maximum(m_old, jnp.max(s, axis=-1, keepdims=True))
            alpha = jnp.exp(m_old - m_new)
            p = jnp.exp(s - m_new)
            l_ref[h] = alpha * l_ref[h] + jnp.sum(p, axis=-1, keepdims=True)
            acc_ref[h] = alpha * acc_ref[h] + jnp.dot(p.astype(BF16), vblk, preferred_element_type=F32)
            m_ref[h] = m_new

    def body(kb, carry):
        step(kb, False)
        return carry

    lax.fori_loop(0, i, body, 0)
    step(i, True)
    o0 = acc_ref[0] / l_ref[0]
    o1 = acc_ref[1] / l_ref[1]
    o_ref[...] = jnp.where(masks[0], o0, o1).astype(o_ref.dtype)


def _attn_prompt(z, nck, tq):
    b, l, _ = z.shape
    hp = ATT_W // LANES
    return pl.pallas_call(
        functools.partial(_attn_kernel, tq=tq),
        grid=(b, hp, l // tq),
        in_specs=[
            pl.BlockSpec((None, tq, LANES), lambda bi, h, i: (bi, i, h)),
            pl.BlockSpec((None, l, LANES), lambda bi, h, i: (bi, 0, hp + h)),
            pl.BlockSpec((None, l, LANES), lambda bi, h, i: (bi, 0, 2 * hp + h)),
            pl.BlockSpec((None, None, 2, l // tq, tq), lambda bi, h, i: (bi, h, 0, 0, 0)),
        ],
        out_specs=pl.BlockSpec((None, tq, LANES), lambda bi, h, i: (bi, i, h)),
        out_shape=jax.ShapeDtypeStruct((b, l, ATT_W), BF16),
        scratch_shapes=[
            pltpu.VMEM((l, LANES), BF16),
            pltpu.VMEM((l, LANES), BF16),
            pltpu.VMEM((2, tq, 1), F32),
            pltpu.VMEM((2, tq, 1), F32),
            pltpu.VMEM((2, tq, LANES), F32),
        ],
        compiler_params=_params("arbitrary", "arbitrary", "arbitrary"),
        name="attn_prompt",
    )(z, z, z, nck)


def _attn_cache_kernel(q_ref, kn_ref, vn_ref, kc_ref, vc_ref, nckc_ref, nckn_ref, o_ref):
    q = q_ref[...]
    lq = q.shape[0]
    masks = _head_masks()
    kc = kc_ref[...].astype(BF16)
    vc = vc_ref[...].astype(BF16)
    kn = kn_ref[...].astype(BF16)
    vn = vn_ref[...].astype(BF16)
    nt_dims = (((1,), (1,)), ((), ()))
    r = lax.broadcasted_iota(jnp.int32, (lq, lq), 0)
    c = lax.broadcasted_iota(jnp.int32, (lq, lq), 1)
    outs = []
    for h in range(2):
        qh = jnp.where(masks[h], q, 0.0).astype(BF16)
        sc = lax.dot_general(qh, kc, nt_dims, preferred_element_type=F32) + nckc_ref[pl.ds(h, 1), :]
        sn = lax.dot_general(qh, kn, nt_dims, preferred_element_type=F32) + nckn_ref[pl.ds(h, 1), :]
        sn = jnp.where(c <= r, sn, -jnp.inf)
        m = jnp.maximum(jnp.max(sc, axis=-1, keepdims=True), jnp.max(sn, axis=-1, keepdims=True))
        pc = jnp.exp(sc - m)
        pn = jnp.exp(sn - m)
        den = jnp.sum(pc, axis=-1, keepdims=True) + jnp.sum(pn, axis=-1, keepdims=True)
        acc = (jnp.dot(pc.astype(BF16), vc, preferred_element_type=F32)
               + jnp.dot(pn.astype(BF16), vn, preferred_element_type=F32))
        outs.append(acc / den)
    o_ref[...] = jnp.where(masks[0], outs[0], outs[1]).astype(o_ref.dtype)


def _attn_sample(z, cache_k, cache_v, layer, nck_c, nck_n):
    b, lq, _ = z.shape
    p = cache_k.shape[2]
    hp = ATT_W // LANES
    return pl.pallas_call(
        _attn_cache_kernel,
        grid=(b, hp),
        in_specs=[
            pl.BlockSpec((None, lq, LANES), lambda bi, h: (bi, 0, h)),
            pl.BlockSpec((None, lq, LANES), lambda bi, h: (bi, 0, hp + h)),
            pl.BlockSpec((None, lq, LANES), lambda bi, h: (bi, 0, 2 * hp + h)),
            pl.BlockSpec((None, None, p, LANES), lambda bi, h: (layer, bi, 0, h)),
            pl.BlockSpec((None, None, p, LANES), lambda bi, h: (layer, bi, 0, h)),
            pl.BlockSpec((None, None, 2, p), lambda bi, h: (bi, h, 0, 0)),
            pl.BlockSpec((None, None, 2, lq), lambda bi, h: (bi, h, 0, 0)),
        ],
        out_specs=pl.BlockSpec((None, lq, LANES), lambda bi, h: (bi, 0, h)),
        out_shape=jax.ShapeDtypeStruct((b, lq, ATT_W), BF16),
        compiler_params=_params("arbitrary", "arbitrary"),
        name="attn_sample",
    )(z, z, z, cache_k, cache_v, nck_c, nck_n)


def _s5_kernel(u_ref, m_ref, w_ref, v_ref, are_ref, aim_ref, d_ref, h0_ref,
               y_ref, send_ref, s_ref, e_ref, sp_ref, *, groups, rows, chunks):
    t = pl.program_id(1)

    @pl.when(t == 0)
    def _():
        s_ref[...] = h0_ref[...]

    def chunk_inputs(g, carry):
        e_ref[g] = jnp.dot(u_ref[g], w_ref[g], preferred_element_type=F32)
        return carry

    lax.fori_loop(0, groups, chunk_inputs, 0)

    a_re = are_ref[...]
    a_im = aim_ref[...]

    def advance(c, carry):
        off = pl.multiple_of(c * rows, rows)
        s = s_ref[...]
        sp_ref[:, pl.ds(off, rows), :] = s.reshape(groups, rows, LANES)
        e = e_ref[:, pl.ds(off, rows), :].reshape(groups * rows, LANES)
        s_ref[...] = a_re * s + a_im * pltpu.roll(s, S5_STATE, 1) + e
        return carry

    lax.fori_loop(0, chunks, advance, 0)

    def outputs(g, carry):
        u = u_ref[g]
        y = jnp.dot(u, m_ref[g], preferred_element_type=F32)
        y = y + jnp.dot(sp_ref[g].astype(BF16), v_ref[g], preferred_element_type=F32)
        y = y + u.astype(F32) * d_ref[g]
        y_ref[g] = y.astype(y_ref.dtype)
        return carry

    lax.fori_loop(0, groups, outputs, 0)

    @pl.when(t == pl.num_programs(1) - 1)
    def _():
        send_ref[...] = s_ref[...]


def _s5(u_r, mats, h0, rows, chunks_per_step, gb):
    m, w, v, a_re, a_im, d_tile = mats
    g, total_rows, width = u_r.shape
    blk = chunks_per_step * rows
    steps = total_rows // blk
    state = 2 * S5_STATE
    return pl.pallas_call(
        functools.partial(_s5_kernel, groups=gb, rows=rows, chunks=chunks_per_step),
        grid=(g // gb, steps),
        in_specs=[
            pl.BlockSpec((gb, blk, width), lambda gi, t: (gi, t, 0)),
            pl.BlockSpec((gb, width, width), lambda gi, t: (gi, 0, 0)),
            pl.BlockSpec((gb, width, state), lambda gi, t: (gi, 0, 0)),
            pl.BlockSpec((gb, state, width), lambda gi, t: (gi, 0, 0)),
            pl.BlockSpec((gb * rows, state), lambda gi, t: (gi, 0)),
            pl.BlockSpec((gb * rows, state), lambda gi, t: (gi, 0)),
            pl.BlockSpec((gb, 1, width), lambda gi, t: (gi, 0, 0)),
            pl.BlockSpec((gb * rows, state), lambda gi, t: (gi, 0)),
        ],
        out_specs=[
            pl.BlockSpec((gb, blk, width), lambda gi, t: (gi, t, 0)),
            pl.BlockSpec((gb * rows, state), lambda gi, t: (gi, 0)),
        ],
        out_shape=[jax.ShapeDtypeStruct(u_r.shape, BF16), jax.ShapeDtypeStruct(h0.shape, F32)],
        scratch_shapes=[
            pltpu.VMEM((gb * rows, state), F32),
            pltpu.VMEM((gb, blk, state), F32),
            pltpu.VMEM((gb, blk, state), F32),
        ],
        compiler_params=_params("arbitrary", "arbitrary"),
        name="s5",
    )(u_r, m, w, v, a_re, a_im, d_tile, h0)


def _s5_matrices(lam_re, lam_im, log_dt, b_re, b_im, c_re, c_im, d_skip, rows):
    g, p = lam_re.shape
    tc = S5_CHUNK
    dt = jnp.exp(log_dt)[:, None]
    k = jnp.arange(tc + 1, dtype=F32)[:, None, None]
    mag = jnp.exp(lam_re * dt * k)
    pw_re = mag * jnp.cos(lam_im * dt * k)
    pw_im = mag * jnp.sin(lam_im * dt * k)
    den = lam_re * lam_re + lam_im * lam_im
    nr = pw_re[1] - 1.0
    f_re = (nr * lam_re + pw_im[1] * lam_im) / den
    f_im = (pw_im[1] * lam_re - nr * lam_im) / den
    bb_re = f_re[..., None] * b_re - f_im[..., None] * b_im
    bb_im = f_re[..., None] * b_im + f_im[..., None] * b_re
    ein = functools.partial(jnp.einsum, precision=HIGHEST)
    cp_re = c_re[None] * pw_re[:tc, :, None, :] - c_im[None] * pw_im[:tc, :, None, :]
    cp_im = c_re[None] * pw_im[:tc, :, None, :] + c_im[None] * pw_re[:tc, :, None, :]
    taps = ein('tghp,gpk->tghk', cp_re, bb_re) - ein('tghp,gpk->tghk', cp_im, bb_im)
    j_idx = jnp.arange(tc)[:, None]
    t_idx = jnp.arange(tc)[None, :]
    lag = t_idx - j_idx
    toe = jnp.where((lag >= 0)[:, :, None, None, None], taps[jnp.clip(lag, 0, tc - 1)], 0.0)
    m = toe.transpose(2, 0, 4, 1, 3).reshape(g, tc * S5_GROUP, tc * S5_GROUP)
    rp_re = pw_re[:tc][::-1][:, :, :, None]
    rp_im = pw_im[:tc][::-1][:, :, :, None]
    w_re = rp_re * bb_re[None] - rp_im * bb_im[None]
    w_im = rp_re * bb_im[None] + rp_im * bb_re[None]
    w = jnp.concatenate([w_re.transpose(1, 0, 3, 2), w_im.transpose(1, 0, 3, 2)], axis=-1)
    w = w.reshape(g, tc * S5_GROUP, 2 * p)
    z_re = c_re[None] * pw_re[1:, :, None, :] - c_im[None] * pw_im[1:, :, None, :]
    z_im = c_re[None] * pw_im[1:, :, None, :] + c_im[None] * pw_re[1:, :, None, :]
    v = jnp.concatenate([z_re.transpose(1, 3, 0, 2), -z_im.transpose(1, 3, 0, 2)], axis=1)
    v = v.reshape(g, 2 * p, tc * S5_GROUP)
    a_re = jnp.concatenate([pw_re[tc], pw_re[tc]], axis=-1)
    a_im = jnp.concatenate([-pw_im[tc], pw_im[tc]], axis=-1)
    a_re = jnp.repeat(a_re, rows, axis=0)
    a_im = jnp.repeat(a_im, rows, axis=0)
    d_tile = jnp.tile(d_skip.reshape(g, 1, S5_GROUP), (1, 1, tc))
    return m.astype(BF16), w.astype(BF16), v.astype(BF16), a_re, a_im, d_tile


def _s5_to_chunks(u, b, l, rows):
    g = u.shape[1] // S5_GROUP
    nc = l // S5_CHUNK
    u = u.astype(BF16).reshape(b, nc, S5_CHUNK, g, S5_GROUP).transpose(3, 1, 0, 2, 4)
    u = u.reshape(g, nc, b, S5_CHUNK * S5_GROUP)
    if rows != b:
        u = jnp.pad(u, ((0, 0), (0, 0), (0, rows - b), (0, 0)))
    return u.reshape(g, nc * rows, S5_CHUNK * S5_GROUP)


def _s5_from_chunks(y_r, b, l, rows):
    g = y_r.shape[0]
    nc = l // S5_CHUNK
    y = y_r.reshape(g, nc, rows, S5_CHUNK, S5_GROUP)[:, :, :b]
    return y.transpose(2, 1, 3, 0, 4).reshape(b * l, g * S5_GROUP)


def _mix_kernel(o_ref, y_ref, ga_ref, gb_ref, wa_ref, wv_ref, wg_ref, out_ref, g5_ref):
    @pl.when(pl.program_id(1) == 0)
    def _():
        g5_ref[...] = _gelu_tanh(y_ref[...].astype(F32)).astype(BF16)

    ya = jnp.dot(o_ref[...], wa_ref[...], preferred_element_type=F32)
    g5 = g5_ref[...]
    yb = (jnp.dot(g5, wv_ref[...], preferred_element_type=F32)
          * _sigmoid(jnp.dot(g5, wg_ref[...], preferred_element_type=F32)))
    out = _sigmoid(ga_ref[...]) * ya + _sigmoid(gb_ref[...]) * yb
    out_ref[...] = out.astype(out_ref.dtype)


def _mix(o, y, z, gate_col, wa, wv, wg, tm, tn):
    t, kdim = o.shape
    n = wa.shape[1]
    ga0 = gate_col // tn
    gb0 = (gate_col + n) // tn
    return pl.pallas_call(
        _mix_kernel,
        grid=(t // tm, n // tn),
        in_specs=[
            pl.BlockSpec((tm, kdim), lambda i, j: (i, 0)),
            pl.BlockSpec((tm, kdim), lambda i, j: (i, 0)),
            pl.BlockSpec((tm, tn), lambda i, j: (i, ga0 + j)),
            pl.BlockSpec((tm, tn), lambda i, j: (i, gb0 + j)),
            pl.BlockSpec((kdim, tn), lambda i, j: (0, j)),
            pl.BlockSpec((kdim, tn), lambda i, j: (0, j)),
            pl.BlockSpec((kdim, tn), lambda i, j: (0, j)),
        ],
        out_specs=pl.BlockSpec((tm, tn), lambda i, j: (i, j)),
        out_shape=jax.ShapeDtypeStruct((t, n), BF16),
        scratch_shapes=[pltpu.VMEM((tm, kdim), BF16)],
        compiler_params=_params("arbitrary", "arbitrary"),
        name="mix",
    )(o, y, z, z, wa, wv, wg)


def _mm_norm_kernel(a_ref, w_ref, x_ref, g_ref, out_ref):
    k = pl.program_id(1)
    part = jnp.dot(a_ref[...], w_ref[...], preferred_element_type=F32)

    @pl.when(k == 0)
    def _():
        out_ref[...] = part

    @pl.when(k != 0)
    def _():
        out_ref[...] += part

    @pl.when(k == pl.num_programs(1) - 1)
    def _():
        m = out_ref[...]
        out_ref[...] = x_ref[...] + (m * _rms_scale(m)) * g_ref[...]


def _mm_norm(a, w, x, g, tm, tk):
    t, kdim = a.shape
    n = w.shape[1]
    return pl.pallas_call(
        _mm_norm_kernel,
        grid=(t // tm, kdim // tk),
        in_specs=[
            pl.BlockSpec((tm, tk), lambda i, k: (i, k)),
            pl.BlockSpec((tk, n), lambda i, k: (k, 0)),
            pl.BlockSpec((tm, n), lambda i, k: (i, 0)),
            pl.BlockSpec((1, n), lambda i, k: (0, 0)),
        ],
        out_specs=pl.BlockSpec((tm, n), lambda i, k: (i, 0)),
        out_shape=jax.ShapeDtypeStruct((t, n), F32),
        compiler_params=_params("arbitrary", "arbitrary"),
        name="mm_norm",
    )(a, w, x, g)


def _ffn_up_kernel(x_ref, g_ref, wa_ref, wb_ref, cw_ref, buf_ref, act_ref, nbuf_ref, h_ref, tail_ref,
                   *, rows_outer, blocks_per_seq):
    if rows_outer:
        i, j = pl.program_id(0), pl.program_id(1)
        fresh_rows = j == 0
    else:
        j, i = pl.program_id(0), pl.program_id(1)
        fresh_rows = True

    def normalise():
        x = x_ref[...]
        h_ref[...] = ((x * _rms_scale(x)) * g_ref[...]).astype(BF16)

    if rows_outer:
        pl.when(fresh_rows)(normalise)
    else:
        normalise()

    if blocks_per_seq > 1:
        @pl.when((i == 0) & (j == 0))
        def _():
            tail_ref[...] = jnp.zeros_like(tail_ref)

    h = h_ref[...]
    a = jnp.dot(h, wa_ref[...], preferred_element_type=F32)
    b = jnp.dot(h, wb_ref[...], preferred_element_type=F32)
    tm = a.shape[0]
    if blocks_per_seq == 1:
        prev = buf_ref[...]
    else:
        prev = jnp.where(i % blocks_per_seq == 0, buf_ref[...], tail_ref[j])
    last = a[tm - SUBLANES:, :]
    tail_ref[j] = last
    nbuf_ref[...] = last
    row = lax.broadcasted_iota(jnp.int32, a.shape, 0)
    p1 = prev[SUBLANES - 1:SUBLANES, :]
    p2 = prev[SUBLANES - 2:SUBLANES - 1, :]
    a1 = jnp.where(row == 0, p1, pltpu.roll(a, 1, 0))
    a2 = jnp.where(row == 0, p2, jnp.where(row == 1, p1, pltpu.roll(a, 2, 0)))
    cw = cw_ref[...]
    conv = a2 * cw[0:1, :] + a1 * cw[1:2, :] + a * cw[2:3, :]
    act_ref[...] = (_gelu_tanh(conv) * b).astype(act_ref.dtype)


def _ffn_up(x, g, w_up, cw, buf, seq_len, tm, tn, rows_outer):
    t, d = x.shape
    f = w_up.shape[1] // 2
    nj = f // tn
    bps = seq_len // tm
    if rows_outer:
        grid = (t // tm, nj)
        ij = lambda a, b: (a, b)
    else:
        grid = (nj, t // tm)
        ij = lambda a, b: (b, a)

    def spec(shape, fn):
        return pl.BlockSpec(shape, lambda a, b: fn(*ij(a, b)))

    return pl.pallas_call(
        functools.partial(_ffn_up_kernel, rows_outer=rows_outer, blocks_per_seq=bps),
        grid=grid,
        in_specs=[
            spec((tm, d), lambda i, j: (i, 0)),
            spec((1, d), lambda i, j: (0, 0)),
            spec((d, tn), lambda i, j: (0, j)),
            spec((d, tn), lambda i, j: (0, nj + j)),
            spec((SUBLANES, tn), lambda i, j: (0, j)),
            spec((None, SUBLANES, tn), lambda i, j: (i // bps, 0, j)),
        ],
        out_specs=[
            spec((tm, tn), lambda i, j: (i, j)),
            spec((None, SUBLANES, tn), lambda i, j: (i, 0, j)),
        ],
        out_shape=[jax.ShapeDtypeStruct((t, f), BF16), jax.ShapeDtypeStruct((t // tm, SUBLANES, f), F32)],
        scratch_shapes=[pltpu.VMEM((tm, d), BF16), pltpu.VMEM((nj, SUBLANES, tn), F32)],
        compiler_params=_params("arbitrary", "arbitrary"),
        name="ffn_up",
    )(x, g, w_up, w_up, cw, buf)


def _pick(t, pref):
    return pref if t % pref == 0 else t


def _trunk(x, weights, caches):
    b, l, d = x.shape
    t = b * l
    depth = weights['w_main'].shape[0]
    n_main = weights['w_main'].shape[2]
    gate_col = n_main - 2 * d
    u_col = 3 * ATT_W
    g_s5 = weights['lam_re'].shape[1]
    f = weights['w_down'].shape[1]
    rows = -(-b // SUBLANES) * SUBLANES
    n_chunks = l // S5_CHUNK
    tm = _pick(t, 1024)
    long_seq = l % tm == 0
    tm_up = tm if long_seq else l
    hp = ATT_W // LANES
    x = x.reshape(t, d)
    st = {k: [] for k in ('k', 'v', 'logf', 're', 'im', 'conv')}
    for layer in range(depth):
        wl = {k: v[layer] for k, v in weights.items()}
        z, logf = _norm_in(x, wl['g_mix_pre'], wl['w_main'], wl['w_f'], wl['b_f'], tm, 512)
        st['k'].append(z[:, ATT_W:2 * ATT_W].reshape(b, l, N_HEADS, HEAD_DIM))
        st['v'].append(z[:, 2 * ATT_W:3 * ATT_W].reshape(b, l, N_HEADS, HEAD_DIM))
        st['logf'].append(logf[:, :N_HEADS].reshape(b, l, N_HEADS))
        z3 = z.reshape(b, l, n_main)
        if caches is None:
            tq = 512
            nck = _neg_cumsum_t(logf.reshape(b, l, LANES))[:, :N_HEADS].reshape(b, hp, 2, l // tq, tq)
            o = _attn_prompt(z3, nck, tq)
            h0 = jnp.zeros((g_s5 * rows, 2 * S5_STATE), F32)
            buf = jnp.zeros((b, SUBLANES, f), F32)
        else:
            past = caches['logf'].shape[2]
            lf_c = jnp.pad(caches['logf'][layer], ((0, 0), (0, 0), (0, LANES - N_HEADS)))
            lf_all = jnp.concatenate([lf_c, logf.reshape(b, l, LANES)], axis=1)
            padded = -(-(past + l) // LANES) * LANES
            lf_all = jnp.pad(lf_all, ((0, 0), (0, padded - past - l), (0, 0)))
            nck_all = _neg_cumsum_t(lf_all)[:, :N_HEADS]
            nck_c = nck_all[:, :, :past].reshape(b, hp, 2, past)
            nck_n = nck_all[:, :, past:past + l].reshape(b, hp, 2, l)
            o = _attn_sample(z3, caches['k'], caches['v'], layer, nck_c, nck_n)
            h0 = jnp.concatenate([caches['ssm_re'][layer], caches['ssm_im'][layer]], axis=-1)
            h0 = jnp.pad(h0.transpose(1, 0, 2), ((0, 0), (0, rows - b), (0, 0))).reshape(g_s5 * rows, 2 * S5_STATE)
            buf = jnp.pad(caches['conv'][layer], ((0, 0), (SUBLANES - (CONV_W - 1), 0), (0, 0)))
        o = o.reshape(t, ATT_W)
        mats = _s5_matrices(wl['lam_re'], wl['lam_im'], wl['log_dt'], wl['b_re'], wl['b_im'],
                            wl['c_re'], wl['c_im'], wl['d_skip'], rows)
        u_r = _s5_to_chunks(z[:, u_col:u_col + g_s5 * S5_GROUP], b, l, rows)
        y_r, s_end = _s5(u_r, mats, h0, rows, min(64, n_chunks), 16)
        y = _s5_from_chunks(y_r, b, l, rows)
        s_end = s_end.reshape(g_s5, rows, 2 * S5_STATE)[:, :b].transpose(1, 0, 2)
        st['re'].append(s_end[..., :S5_STATE])
        st['im'].append(s_end[..., S5_STATE:])
        mix_in = _mix(o, y, z, gate_col, wl['w_att_proj'], wl['w_glu_v'], wl['w_glu_g'], tm, 512)
        x = _mm_norm(mix_in, wl['w_out'], x, wl['g_mix_post'], tm, 512)
        act, nbuf = _ffn_up(x, wl['g_ffn_pre'], wl['w_up'], wl['conv_w'], buf, l, tm_up, 512, long_seq)
        nbuf = nbuf.reshape(b, l // tm_up, SUBLANES, f)[:, -1]
        st['conv'].append(nbuf[:, SUBLANES - (CONV_W - 1):])
        x = _mm_norm(act, wl['w_down'], x, wl['g_ffn_post'], tm, 512)
    stacked = [jnp.stack(st[k]) for k in ('k', 'v', 'logf', 're', 'im', 'conv')]
    return x.reshape(b, l, d), stacked


def kernel(x_prompt, x_sample, cache_k, cache_v, cache_logf, state_ssm_re, state_ssm_im, state_conv,
           g_mix_pre, w_in, b_f, lam_re, lam_im, log_dt, b_re, b_im, c_re, c_im, d_skip,
           w_att_proj, w_glu_v, w_glu_g, w_out, g_mix_post, g_ffn_pre, w_up, conv_w, w_down, g_ffn_post):
    depth, d, _ = w_in.shape
    s5_w = d_skip.shape[1]
    f_lo, f_hi = 3 * ATT_W, 3 * ATT_W + N_HEADS
    scale = HEAD_DIM ** -0.5
    w_main = jnp.concatenate([w_in[:, :, :ATT_W] * scale, w_in[:, :, ATT_W:f_lo], w_in[:, :, f_hi:]], axis=2)
    weights = {
        'w_main': w_main.astype(BF16),
        'w_f': jnp.pad(w_in[:, :, f_lo:f_hi], ((0, 0), (0, 0), (0, LANES - N_HEADS))).astype(BF16),
        'b_f': jnp.pad(b_f, ((0, 0), (0, LANES - N_HEADS))).reshape(depth, 1, LANES),
        'g_mix_pre': g_mix_pre.reshape(depth, 1, d),
        'g_mix_post': g_mix_post.reshape(depth, 1, d),
        'g_ffn_pre': g_ffn_pre.reshape(depth, 1, d),
        'g_ffn_post': g_ffn_post.reshape(depth, 1, d),
        'lam_re': lam_re, 'lam_im': lam_im, 'log_dt': log_dt, 'b_re': b_re, 'b_im': b_im,
        'c_re': c_re, 'c_im': c_im, 'd_skip': d_skip,
        'w_att_proj': w_att_proj.astype(BF16),
        'w_glu_v': w_glu_v.astype(BF16),
        'w_glu_g': w_glu_g.astype(BF16),
        'w_out': w_out.astype(BF16),
        'w_up': w_up.astype(BF16),
        'conv_w': jnp.pad(conv_w, ((0, 0), (0, SUBLANES - CONV_W), (0, 0))),
        'w_down': w_down.astype(BF16),
    }
    assert s5_w == lam_re.shape[1] * S5_GROUP
    sb, sp = cache_k.shape[1], cache_k.shape[2]
    caches = {
        'k': cache_k.reshape(depth, sb, sp, ATT_W),
        'v': cache_v.reshape(depth, sb, sp, ATT_W),
        'logf': cache_logf, 'ssm_re': state_ssm_re, 'ssm_im': state_ssm_im, 'conv': state_conv,
    }
    y_prompt, p_st = _trunk(x_prompt, weights, None)
    y_sample, s_st = _trunk(x_sample, weights, caches)
    return (y_prompt, y_sample, *p_st, *s_st)
```

```python
import functools
import math

import jax
import jax.numpy as jnp
from jax import lax
from jax.experimental import pallas as pl
from jax.experimental.pallas import tpu as pltpu

F32 = jnp.float32
BF16 = jnp.bfloat16
HIGHEST = lax.Precision.HIGHEST

RMS_EPS = 1e-6
N_HEADS = 16
HEAD_DIM = 64
ATT_W = N_HEADS * HEAD_DIM
S5_GROUP = 16
S5_STATE = 64
CONV_W = 3
LANES = 128
SUBLANES = 8
S5_CHUNK = 16
VMEM_LIMIT_BYTES = 56 * 1024 * 1024


def _params(*sem):
    return pltpu.CompilerParams(dimension_semantics=sem, vmem_limit_bytes=VMEM_LIMIT_BYTES)


def _gelu_tanh(x):
    return 0.5 * x * (1.0 + jnp.tanh(0.7978845608028654 * (x + 0.044715 * (x * x * x))))


def _sigmoid(x):
    return 1.0 / (1.0 + jnp.exp(-x))


def _log_sigmoid(x):
    return jnp.minimum(x, 0.0) - jnp.log1p(jnp.exp(-jnp.abs(x)))


def _rms_scale(x):
    return lax.rsqrt(jnp.mean(x * x, axis=-1, keepdims=True) + RMS_EPS)


def _norm_in_kernel(x_ref, g_ref, w_ref, wf_ref, bf_ref, z_ref, lf_ref, h_ref):
    @pl.when(pl.program_id(1) == 0)
    def _():
        x = x_ref[...]
        h = ((x * _rms_scale(x)) * g_ref[...]).astype(BF16)
        h_ref[...] = h
        fl = jnp.dot(h, wf_ref[...], preferred_element_type=F32) + bf_ref[...]
        lf_ref[...] = _log_sigmoid(fl)

    z_ref[...] = jnp.dot(h_ref[...], w_ref[...], preferred_element_type=F32)


def _norm_in(x, g, w, wf, bf, tm, tn):
    t, d = x.shape
    n = w.shape[1]
    return pl.pallas_call(
        _norm_in_kernel,
        grid=(t // tm, n // tn),
        in_specs=[
            pl.BlockSpec((tm, d), lambda i, j: (i, 0)),
            pl.BlockSpec((1, d), lambda i, j: (0, 0)),
            pl.BlockSpec((d, tn), lambda i, j: (0, j)),
            pl.BlockSpec((d, LANES), lambda i, j: (0, 0)),
            pl.BlockSpec((1, LANES), lambda i, j: (0, 0)),
        ],
        out_specs=[
            pl.BlockSpec((tm, tn), lambda i, j: (i, j)),
            pl.BlockSpec((tm, LANES), lambda i, j: (i, 0)),
        ],
        out_shape=[jax.ShapeDtypeStruct((t, n), F32), jax.ShapeDtypeStruct((t, LANES), F32)],
        scratch_shapes=[pltpu.VMEM((tm, d), BF16)],
        compiler_params=_params("arbitrary", "arbitrary"),
        name="norm_in",
    )(x, g, w, wf, bf)


def _cumsum_kernel(x_ref, o_ref, carry_ref):
    @pl.when(pl.program_id(1) == 0)
    def _():
        carry_ref[...] = jnp.zeros_like(carry_ref)

    x = x_ref[...]
    row = lax.broadcasted_iota(jnp.int32, (LANES, LANES), 0)
    col = lax.broadcasted_iota(jnp.int32, (LANES, LANES), 1)
    upper = (row <= col).astype(F32)
    ones = jnp.ones((LANES, LANES), F32)
    tn_dims = (((0,), (0,)), ((), ()))
    ct = lax.dot_general(x, upper, tn_dims, precision=HIGHEST, preferred_element_type=F32)
    tot = lax.dot_general(x, ones, tn_dims, precision=HIGHEST, preferred_element_type=F32)
    o_ref[...] = -(ct + carry_ref[...])
    carry_ref[...] += tot


def _neg_cumsum_t(x):
    b, l, _ = x.shape
    return pl.pallas_call(
        _cumsum_kernel,
        grid=(b, l // LANES),
        in_specs=[pl.BlockSpec((None, LANES, LANES), lambda i, j: (i, j, 0))],
        out_specs=pl.BlockSpec((None, LANES, LANES), lambda i, j: (i, 0, j)),
        out_shape=jax.ShapeDtypeStruct((b, LANES, l), F32),
        scratch_shapes=[pltpu.VMEM((LANES, LANES), F32)],
        compiler_params=_params("arbitrary", "arbitrary"),
        name="cumsum",
    )(x)


def _cumsum_rows_kernel(x_ref, o_ref, carry_ref):
    @pl.when(pl.program_id(1) == 0)
    def _():
        carry_ref[...] = jnp.zeros_like(carry_ref)

    n = x_ref.shape[0]
    row = lax.broadcasted_iota(jnp.int32, (n, n), 0)
    col = lax.broadcasted_iota(jnp.int32, (n, n), 1)
    lower = (col <= row).astype(F32)
    c = jnp.dot(lower, x_ref[...], precision=HIGHEST, preferred_element_type=F32) + carry_ref[...]
    o_ref[...] = c
    carry_ref[...] = c[n - 1:n, :]


def _cumsum_rows(x, blk):
    b, l, w = x.shape
    return pl.pallas_call(
        _cumsum_rows_kernel,
        grid=(b, l // blk),
        in_specs=[pl.BlockSpec((None, blk, w), lambda i, j: (i, j, 0))],
        out_specs=pl.BlockSpec((None, blk, w), lambda i, j: (i, j, 0)),
        out_shape=jax.ShapeDtypeStruct((b, l, w), F32),
        scratch_shapes=[pltpu.VMEM((1, w), F32)],
        compiler_params=_params("arbitrary", "arbitrary"),
        name="cumsum_rows",
    )(x)


def _head_masks():
    lane = lax.broadcasted_iota(jnp.int32, (1, LANES), 1)
    return lane < HEAD_DIM, lane >= HEAD_DIM


BIAS_PARTS = 3


def _attn_kernel(q_ref, k_ref, v_ref, c_ref, o_ref, kx_ref, vt_ref, m_ref, acc_ref, *, tq):
    hp = pl.program_id(1)
    i = pl.program_id(2)
    n_blk = k_ref.shape[0] // tq
    lane = lax.broadcasted_iota(jnp.int32, (1, LANES), 1)
    own = [(lane >= h * HEAD_DIM) & (lane < (h + 1) * HEAD_DIM) for h in range(2)]
    spare = [(1 - h) * HEAD_DIM for h in range(2)]

    @pl.when(i == 0)
    def _():
        er = lax.broadcasted_iota(jnp.int32, (LANES, LANES), 0)
        ec = lax.broadcasted_iota(jnp.int32, (LANES, LANES), 1)
        row = lax.broadcasted_iota(jnp.int32, (LANES, tq), 0)

        def prep(j, carry):
            r0 = pl.multiple_of(j * tq, tq)
            kblk = k_ref[pl.ds(r0, tq), :]
            nc = -c_ref[pl.ds(r0, tq), :]
            parts = []
            rem = nc
            for _ in range(BIAS_PARTS):
                piece = rem.astype(BF16)
                parts.append(piece)
                rem = rem - piece.astype(F32)
            vt = v_ref[pl.ds(r0, tq), :].T
            for h in range(2):
                bias = jnp.zeros((tq, LANES), F32)
                for n, piece in enumerate(parts):
                    sel = ((er == 2 * hp + h) & (ec == spare[h] + n)).astype(BF16)
                    bias = bias + jnp.dot(piece, sel, preferred_element_type=F32)
                kx_ref[h, pl.ds(r0, tq), :] = (jnp.where(own[h], kblk, 0.0) + bias).astype(BF16)
                vth = vt if h == 0 else pltpu.roll(vt, HEAD_DIM, 0)
                vt_ref[h, j] = jnp.where(row < HEAD_DIM, vth, 1.0).astype(BF16)
            return carry

        lax.fori_loop(0, n_blk, prep, 0)

    q = q_ref[...]
    qx = []
    for h in range(2):
        ones = (lane >= spare[h]) & (lane < spare[h] + BIAS_PARTS)
        qx.append((jnp.where(own[h], q, 0.0) + jnp.where(ones, 1.0, 0.0)).astype(BF16))
    m_ref[...] = jnp.full_like(m_ref, -jnp.inf)
    acc_ref[...] = jnp.zeros_like(acc_ref)
    nt_dims = (((1,), (1,)), ((), ()))

    def step(kb, diagonal):
        start = pl.multiple_of(kb * tq, tq)
        sts = [lax.dot_general(kx_ref[h, pl.ds(start, tq), :], qx[h], nt_dims,
                               preferred_element_type=F32) for h in range(2)]
        if diagonal:
            key = lax.broadcasted_iota(jnp.int32, (tq, tq), 0)
            qry = lax.broadcasted_iota(jnp.int32, (tq, tq), 1)
            sts = [jnp.where(key <= qry, st, -jnp.inf) for st in sts]
        m_old = [m_ref[h] for h in range(2)]
        m_new = [jnp.maximum(m_old[h], jnp.max(sts[h], axis=0, keepdims=True)) for h in range(2)]
        pts = [jnp.exp(sts[h] - m_new[h]).astype(BF16) for h in range(2)]
        pvs = [jnp.dot(vt_ref[h, kb], pts[h], preferred_element_type=F32) for h in range(2)]
        for h in range(2):
            acc_ref[h] = jnp.exp(m_old[h] - m_new[h]) * acc_ref[h] + pvs[h]
            m_ref[h] = m_new[h]

    def body(kb, carry):
        step(kb, False)
        return carry

    lax.fori_loop(0, i, body, 0)
    step(i, True)
    outs = []
    for h in range(2):
        a = acc_ref[h]
        outs.append(a[:HEAD_DIM] * (1.0 / a[HEAD_DIM:HEAD_DIM + 1]))
    o_ref[...] = jnp.concatenate(outs, axis=0).T.astype(o_ref.dtype)


def _attn_prompt(z, c, tq):
    b, l, _ = z.shape
    hp = ATT_W // LANES
    return pl.pallas_call(
        functools.partial(_attn_kernel, tq=tq),
        grid=(b, hp, l // tq),
        in_specs=[
            pl.BlockSpec((None, tq, LANES), lambda bi, h, i: (bi, i, h)),
            pl.BlockSpec((None, l, LANES), lambda bi, h, i: (bi, 0, hp + h)),
            pl.BlockSpec((None, l, LANES), lambda bi, h, i: (bi, 0, 2 * hp + h)),
            pl.BlockSpec((None, l, LANES), lambda bi, h, i: (bi, 0, 0)),
        ],
        out_specs=pl.BlockSpec((None, tq, LANES), lambda bi, h, i: (bi, i, h)),
        out_shape=jax.ShapeDtypeStruct((b, l, ATT_W), BF16),
        scratch_shapes=[
            pltpu.VMEM((2, l, LANES), BF16),
            pltpu.VMEM((2, l // tq, LANES, tq), BF16),
            pltpu.VMEM((2, 1, tq), F32),
            pltpu.VMEM((2, LANES, tq), F32),
        ],
        compiler_params=_params("arbitrary", "arbitrary", "arbitrary"),
        name="attn_prompt",
    )(z, z, z, c)


def _attn_cache_kernel(q_ref, kn_ref, vn_ref, kc_ref, vc_ref, nckc_ref, nckn_ref, o_ref):
    q = q_ref[...]
    lq = q.shape[0]
    masks = _head_masks()
    kc = kc_ref[...].astype(BF16)
    vc = vc_ref[...].astype(BF16)
    kn = kn_ref[...].astype(BF16)
    vn = vn_ref[...].astype(BF16)
    nt_dims = (((1,), (1,)), ((), ()))
    r = lax.broadcasted_iota(jnp.int32, (lq, lq), 0)
    c = lax.broadcasted_iota(jnp.int32, (lq, lq), 1)
    outs = []
    for h in range(2):
        qh = jnp.where(masks[h], q, 0.0).astype(BF16)
        sc = lax.dot_general(qh, kc, nt_dims, preferred_element_type=F32) + nckc_ref[pl.ds(h, 1), :]
        sn = lax.dot_general(qh, kn, nt_dims, preferred_element_type=F32) + nckn_ref[pl.ds(h, 1), :]
        sn = jnp.where(c <= r, sn, -jnp.inf)
        m = jnp.maximum(jnp.max(sc, axis=-1, keepdims=True), jnp.max(sn, axis=-1, keepdims=True))
        pc = jnp.exp(sc - m)
        pn = jnp.exp(sn - m)
        den = jnp.sum(pc, axis=-1, keepdims=True) + jnp.sum(pn, axis=-1, keepdims=True)
        acc = (jnp.dot(pc.astype(BF16), vc, preferred_element_type=F32)
               + jnp.dot(pn.astype(BF16), vn, preferred_element_type=F32))
        outs.append(acc / den)
    o_ref[...] = jnp.where(masks[0], outs[0], outs[1]).astype(o_ref.dtype)


def _attn_sample(z, cache_k, cache_v, layer, nck_c, nck_n):
    b, lq, _ = z.shape
    p = cache_k.shape[2]
    hp = ATT_W // LANES
    return pl.pallas_call(
        _attn_cache_kernel,
        grid=(b, hp),
        in_specs=[
            pl.BlockSpec((None, lq, LANES), lambda bi, h: (bi, 0, h)),
            pl.BlockSpec((None, lq, LANES), lambda bi, h: (bi, 0, hp + h)),
            pl.BlockSpec((None, lq, LANES), lambda bi, h: (bi, 0, 2 * hp + h)),
            pl.BlockSpec((None, None, p, LANES), lambda bi, h: (layer, bi, 0, h)),
            pl.BlockSpec((None, None, p, LANES), lambda bi, h: (layer, bi, 0, h)),
            pl.BlockSpec((None, None, 2, p), lambda bi, h: (bi, h, 0, 0)),
            pl.BlockSpec((None, None, 2, lq), lambda bi, h: (bi, h, 0, 0)),
        ],
        out_specs=pl.BlockSpec((None, lq, LANES), lambda bi, h: (bi, 0, h)),
        out_shape=jax.ShapeDtypeStruct((b, lq, ATT_W), BF16),
        compiler_params=_params("arbitrary", "arbitrary"),
        name="attn_sample",
    )(z, z, z, cache_k, cache_v, nck_c, nck_n)


S5_TILE_GROUPS = LANES // S5_GROUP
S5_TILE_STATE = 2 * S5_TILE_GROUPS * S5_STATE


def _s5_kernel(u_ref, bw_ref, bd_ref, bv_ref, are_ref, aim_ref, d_ref, h0_ref,
               y_ref, send_ref, s_ref, e_ref, sp_ref, *, nb, cb):
    t = pl.program_id(1)
    tc = S5_CHUNK
    rows = nb * cb

    @pl.when(t == 0)
    def _():
        s_ref[...] = h0_ref[...]

    def tokens(j):
        if cb == 1:
            return u_ref[:, j, :]
        return u_ref[:, pl.ds(j, cb, stride=tc), :].reshape(rows, LANES)

    us = [tokens(j) for j in range(tc)]
    ucat = jnp.concatenate([u.astype(BF16) for u in us], axis=1)
    e = jnp.dot(ucat, bw_ref[...], preferred_element_type=F32)
    e_ref[...] = e.reshape(nb, cb, S5_TILE_STATE)

    a_re = are_ref[...]
    a_im = aim_ref[...]

    def advance(c, carry):
        for b in range(nb):
            s = s_ref[b]
            sp_ref[b, pl.ds(c, 1), :] = s
            s_ref[b] = a_re * s + a_im * pltpu.roll(s, S5_TILE_STATE // 2, 1) + e_ref[b, pl.ds(c, 1), :]
        return carry

    lax.fori_loop(0, cb, advance, 0)

    sp = sp_ref[...].reshape(rows, S5_TILE_STATE).astype(BF16)
    d = d_ref[...]
    y_state = jnp.dot(sp, bv_ref[...], preferred_element_type=F32)
    pair = 2 * LANES
    for k in range(tc // 2):
        kk = pair * (k + 1)
        yk = jnp.dot(ucat[:, :kk], bd_ref[:kk, k * pair:(k + 1) * pair], preferred_element_type=F32)
        yk = yk + y_state[:, k * pair:(k + 1) * pair]
        for jj in range(2):
            jo = 2 * k + jj
            y = yk[:, jj * LANES:(jj + 1) * LANES] + us[jo] * d
            if cb == 1:
                y_ref[:, jo, :] = y
            else:
                y_ref[:, pl.ds(jo, cb, stride=tc), :] = y.reshape(nb, cb, LANES)

    @pl.when(t == pl.num_programs(1) - 1)
    def _():
        send_ref[...] = s_ref[...]


def _s5(z3, u_col, mats, h0, cb):
    bw, bd, bv, a_re, a_im, d = mats
    nb, l, _ = z3.shape
    tiles = bw.shape[0]
    tc = S5_CHUNK
    blk = tc * cb
    col0 = u_col // LANES
    return pl.pallas_call(
        functools.partial(_s5_kernel, nb=nb, cb=cb),
        grid=(tiles, l // blk),
        in_specs=[
            pl.BlockSpec((nb, blk, LANES), lambda g, t: (0, t, col0 + g)),
            pl.BlockSpec((None, tc * LANES, S5_TILE_STATE), lambda g, t: (g, 0, 0)),
            pl.BlockSpec((None, tc * LANES, tc * LANES), lambda g, t: (g, 0, 0)),
            pl.BlockSpec((None, S5_TILE_STATE, tc * LANES), lambda g, t: (g, 0, 0)),
            pl.BlockSpec((None, 1, S5_TILE_STATE), lambda g, t: (g, 0, 0)),
            pl.BlockSpec((None, 1, S5_TILE_STATE), lambda g, t: (g, 0, 0)),
            pl.BlockSpec((None, 1, LANES), lambda g, t: (g, 0, 0)),
            pl.BlockSpec((nb, None, 1, S5_TILE_STATE), lambda g, t: (0, g, 0, 0)),
        ],
        out_specs=[
            pl.BlockSpec((nb, blk, LANES), lambda g, t: (0, t, g)),
            pl.BlockSpec((nb, None, 1, S5_TILE_STATE), lambda g, t: (0, g, 0, 0)),
        ],
        out_shape=[jax.ShapeDtypeStruct((nb, l, tiles * LANES), F32), jax.ShapeDtypeStruct(h0.shape, F32)],
        scratch_shapes=[
            pltpu.VMEM((nb, 1, S5_TILE_STATE), F32),
            pltpu.VMEM((nb, cb, S5_TILE_STATE), F32),
            pltpu.VMEM((nb, cb, S5_TILE_STATE), F32),
        ],
        compiler_params=_params("arbitrary", "arbitrary"),
        name="s5",
    )(z3, bw, bd, bv, a_re, a_im, d, h0)


def _s5_matrices(lam_re, lam_im, log_dt, b_re, b_im, c_re, c_im, d_skip):
    g, p = lam_re.shape
    tc = S5_CHUNK
    dt = jnp.exp(log_dt)[:, None]
    k = jnp.arange(tc + 1, dtype=F32)[:, None, None]
    mag = jnp.exp(lam_re * dt * k)
    pw_re = mag * jnp.cos(lam_im * dt * k)
    pw_im = mag * jnp.sin(lam_im * dt * k)
    den = lam_re * lam_re + lam_im * lam_im
    nr = pw_re[1] - 1.0
    f_re = (nr * lam_re + pw_im[1] * lam_im) / den
    f_im = (pw_im[1] * lam_re - nr * lam_im) / den
    bb_re = f_re[..., None] * b_re - f_im[..., None] * b_im
    bb_im = f_re[..., None] * b_im + f_im[..., None] * b_re
    ein = functools.partial(jnp.einsum, precision=HIGHEST)
    cp_re = c_re[None] * pw_re[:tc, :, None, :] - c_im[None] * pw_im[:tc, :, None, :]
    cp_im = c_re[None] * pw_im[:tc, :, None, :] + c_im[None] * pw_re[:tc, :, None, :]
    taps = ein('tghp,gpk->tghk', cp_re, bb_re) - ein('tghp,gpk->tghk', cp_im, bb_im)
    k_rev = (tc - 1) - jnp.arange(tc, dtype=F32)[:, None, None]
    mag_rev = jnp.exp(lam_re * dt * k_rev)
    rp_re = (mag_rev * jnp.cos(lam_im * dt * k_rev))[:, :, :, None]
    rp_im = (mag_rev * jnp.sin(lam_im * dt * k_rev))[:, :, :, None]
    w_re = rp_re * bb_re[None] - rp_im * bb_im[None]
    w_im = rp_re * bb_im[None] + rp_im * bb_re[None]
    z_re = c_re[None] * pw_re[1:, :, None, :] - c_im[None] * pw_im[1:, :, None, :]
    z_im = c_re[None] * pw_im[1:, :, None, :] + c_im[None] * pw_re[1:, :, None, :]
    gl = S5_TILE_GROUPS
    tiles = g // gl
    h = S5_GROUP
    eye = jnp.eye(gl, dtype=F32)
    tp = taps.reshape(tc, tiles, gl, h, h).transpose(1, 0, 2, 4, 3)
    bd = tp[:, :, :, :, None, :] * eye[None, None, :, None, :, None]
    bd = bd.reshape(tiles, tc, LANES, LANES).astype(BF16)
    lag = jnp.arange(tc)[None, :] - jnp.arange(tc)[:, None]
    bd = jnp.where((lag >= 0)[None, :, :, None, None], bd[:, jnp.clip(lag, 0, tc - 1)], 0)
    bd = bd.transpose(0, 1, 3, 2, 4).reshape(tiles, tc * LANES, tc * LANES)
    ws = jnp.stack([w_re, w_im]).reshape(2, tc, tiles, gl, p, h).transpose(2, 1, 3, 5, 0, 4)
    bw = ws[:, :, :, :, :, None, :] * eye[None, None, :, None, None, :, None]
    bw = bw.reshape(tiles, tc * LANES, S5_TILE_STATE)
    zs = jnp.stack([z_re, -z_im]).reshape(2, tc, tiles, gl, h, p).transpose(2, 0, 3, 5, 1, 4)
    bv = zs[:, :, :, :, :, None, :] * eye[None, None, :, None, None, :, None]
    bv = bv.reshape(tiles, S5_TILE_STATE, tc * LANES)
    ar = pw_re[tc].reshape(tiles, gl * p)
    ai = pw_im[tc].reshape(tiles, gl * p)
    a_re = jnp.concatenate([ar, ar], axis=-1)[:, None, :]
    a_im = jnp.concatenate([-ai, ai], axis=-1)[:, None, :]
    d = d_skip.reshape(tiles, 1, LANES)
    return bw.astype(BF16), bd.astype(BF16), bv.astype(BF16), a_re, a_im, d


def _s5_pack_state(re, im):
    b, g, p = re.shape
    tiles = g // S5_TILE_GROUPS
    return jnp.concatenate([re.reshape(b, tiles, 1, -1), im.reshape(b, tiles, 1, -1)], axis=-1)


def _s5_unpack_state(s, g):
    b = s.shape[0]
    half = S5_TILE_STATE // 2
    return s[..., :half].reshape(b, g, S5_STATE), s[..., half:].reshape(b, g, S5_STATE)


def _mix_kernel(o_ref, y_ref, ga_ref, gb_ref, wa_ref, wv_ref, wg_ref, out_ref, g5_ref):
    @pl.when(pl.program_id(1) == 0)
    def _():
        g5_ref[...] = _gelu_tanh(y_ref[...].astype(F32)).astype(BF16)

    ya = jnp.dot(o_ref[...], wa_ref[...], preferred_element_type=F32)
    g5 = g5_ref[...]
    yb = (jnp.dot(g5, wv_ref[...], preferred_element_type=F32)
          * _sigmoid(jnp.dot(g5, wg_ref[...], preferred_element_type=F32)))
    out = _sigmoid(ga_ref[...]) * ya + _sigmoid(gb_ref[...]) * yb
    out_ref[...] = out.astype(out_ref.dtype)


def _mix(o, y, z, gate_col, wa, wv, wg, tm, tn):
    t, kdim = o.shape
    n = wa.shape[1]
    ga0 = gate_col // tn
    gb0 = (gate_col + n) // tn
    return pl.pallas_call(
        _mix_kernel,
        grid=(t // tm, n // tn),
        in_specs=[
            pl.BlockSpec((tm, kdim), lambda i, j: (i, 0)),
            pl.BlockSpec((tm, kdim), lambda i, j: (i, 0)),
            pl.BlockSpec((tm, tn), lambda i, j: (i, ga0 + j)),
            pl.BlockSpec((tm, tn), lambda i, j: (i, gb0 + j)),
            pl.BlockSpec((kdim, tn), lambda i, j: (0, j)),
            pl.BlockSpec((kdim, tn), lambda i, j: (0, j)),
            pl.BlockSpec((kdim, tn), lambda i, j: (0, j)),
        ],
        out_specs=pl.BlockSpec((tm, tn), lambda i, j: (i, j)),
        out_shape=jax.ShapeDtypeStruct((t, n), BF16),
        scratch_shapes=[pltpu.VMEM((tm, kdim), BF16)],
        compiler_params=_params("arbitrary", "arbitrary"),
        name="mix",
    )(o, y, z, z, wa, wv, wg)


def _mm_norm_kernel(a_ref, w_ref, x_ref, g_ref, out_ref):
    k = pl.program_id(1)
    part = jnp.dot(a_ref[...], w_ref[...], preferred_element_type=F32)

    @pl.when(k == 0)
    def _():
        out_ref[...] = part

    @pl.when(k != 0)
    def _():
        out_ref[...] += part

    @pl.when(k == pl.num_programs(1) - 1)
    def _():
        m = out_ref[...]
        out_ref[...] = x_ref[...] + (m * _rms_scale(m)) * g_ref[...]


def _mm_norm(a, w, x, g, tm, tk):
    t, kdim = a.shape
    n = w.shape[1]
    return pl.pallas_call(
        _mm_norm_kernel,
        grid=(t // tm, kdim // tk),
        in_specs=[
            pl.BlockSpec((tm, tk), lambda i, k: (i, k)),
            pl.BlockSpec((tk, n), lambda i, k: (k, 0)),
            pl.BlockSpec((tm, n), lambda i, k: (i, 0)),
            pl.BlockSpec((1, n), lambda i, k: (0, 0)),
        ],
        out_specs=pl.BlockSpec((tm, n), lambda i, k: (i, 0)),
        out_shape=jax.ShapeDtypeStruct((t, n), F32),
        compiler_params=_params("arbitrary", "arbitrary"),
        name="mm_norm",
    )(a, w, x, g)


def _ffn_up_kernel(x_ref, g_ref, wa_ref, wb_ref, cw_ref, buf_ref, act_ref, nbuf_ref, h_ref, tail_ref,
                   *, rows_outer, blocks_per_seq):
    if rows_outer:
        i, j = pl.program_id(0), pl.program_id(1)
        fresh_rows = j == 0
    else:
        j, i = pl.program_id(0), pl.program_id(1)
        fresh_rows = True

    def normalise():
        x = x_ref[...]
        h_ref[...] = ((x * _rms_scale(x)) * g_ref[...]).astype(BF16)

    if rows_outer:
        pl.when(fresh_rows)(normalise)
    else:
        normalise()

    if blocks_per_seq > 1:
        @pl.when((i == 0) & (j == 0))
        def _():
            tail_ref[...] = jnp.zeros_like(tail_ref)

    h = h_ref[...]
    a = jnp.dot(h, wa_ref[...], preferred_element_type=F32)
    b = jnp.dot(h, wb_ref[...], preferred_element_type=F32)
    tm = a.shape[0]
    if blocks_per_seq == 1:
        prev = buf_ref[...]
    else:
        prev = jnp.where(i % blocks_per_seq == 0, buf_ref[...], tail_ref[j])
    last = a[tm - SUBLANES:, :]
    tail_ref[j] = last
    nbuf_ref[...] = last
    row = lax.broadcasted_iota(jnp.int32, a.shape, 0)
    p1 = prev[SUBLANES - 1:SUBLANES, :]
    p2 = prev[SUBLANES - 2:SUBLANES - 1, :]
    a1 = jnp.where(row == 0, p1, pltpu.roll(a, 1, 0))
    a2 = jnp.where(row == 0, p2, jnp.where(row == 1, p1, pltpu.roll(a, 2, 0)))
    cw = cw_ref[...]
    conv = a2 * cw[0:1, :] + a1 * cw[1:2, :] + a * cw[2:3, :]
    act_ref[...] = (_gelu_tanh(conv) * b).astype(act_ref.dtype)


def _ffn_up(x, g, w_up, cw, buf, seq_len, tm, tn, rows_outer):
    t, d = x.shape
    f = w_up.shape[1] // 2
    nj = f // tn
    bps = seq_len // tm
    if rows_outer:
        grid = (t // tm, nj)
        ij = lambda a, b: (a, b)
    else:
        grid = (nj, t // tm)
        ij = lambda a, b: (b, a)

    def spec(shape, fn):
        return pl.BlockSpec(shape, lambda a, b: fn(*ij(a, b)))

    return pl.pallas_call(
        functools.partial(_ffn_up_kernel, rows_outer=rows_outer, blocks_per_seq=bps),
        grid=grid,
        in_specs=[
            spec((tm, d), lambda i, j: (i, 0)),
            spec((1, d), lambda i, j: (0, 0)),
            spec((d, tn), lambda i, j: (0, j)),
            spec((d, tn), lambda i, j: (0, nj + j)),
            spec((SUBLANES, tn), lambda i, j: (0, j)),
            spec((None, SUBLANES, tn), lambda i, j: (i // bps, 0, j)),
        ],
        out_specs=[
            spec((tm, tn), lambda i, j: (i, j)),
            spec((None, SUBLANES, tn), lambda i, j: (i, 0, j)),
        ],
        out_shape=[jax.ShapeDtypeStruct((t, f), BF16), jax.ShapeDtypeStruct((t // tm, SUBLANES, f), F32)],
        scratch_shapes=[pltpu.VMEM((tm, d), BF16), pltpu.VMEM((nj, SUBLANES, tn), F32)],
        compiler_params=_params("arbitrary", "arbitrary"),
        name="ffn_up",
    )(x, g, w_up, w_up, cw, buf)


def _pick(t, pref):
    return pref if t % pref == 0 else t


def _trunk(x, weights, caches):
    b, l, d = x.shape
    t = b * l
    depth = weights['w_main'].shape[0]
    n_main = weights['w_main'].shape[2]
    gate_col = n_main - 2 * d
    u_col = 3 * ATT_W
    g_s5 = weights['lam_re'].shape[1]
    f = weights['w_down'].shape[1]
    rows = -(-b // SUBLANES) * SUBLANES
    n_chunks = l // S5_CHUNK
    tm = _pick(t, 1024)
    long_seq = l % tm == 0
    tm_up = tm if long_seq else l
    hp = ATT_W // LANES
    x = x.reshape(t, d)
    st = {k: [] for k in ('k', 'v', 'logf', 're', 'im', 'conv')}
    for layer in range(depth):
        wl = {k: v[layer] for k, v in weights.items()}
        z, logf = _norm_in(x, wl['g_mix_pre'], wl['w_main'], wl['w_f'], wl['b_f'], tm, 512)
        st['k'].append(z[:, ATT_W:2 * ATT_W].reshape(b, l, N_HEADS, HEAD_DIM))
        st['v'].append(z[:, 2 * ATT_W:3 * ATT_W].reshape(b, l, N_HEADS, HEAD_DIM))
        st['logf'].append(logf[:, :N_HEADS].reshape(b, l, N_HEADS))
        z3 = z.reshape(b, l, n_main)
        if caches is None:
            c = _cumsum_rows(logf.reshape(b, l, LANES), 512)
            o = _attn_prompt(z3, c, 512)
            h0 = jnp.zeros((b, g_s5 // S5_TILE_GROUPS, 1, S5_TILE_STATE), F32)
            buf = jnp.zeros((b, SUBLANES, f), F32)
        else:
            past = caches['logf'].shape[2]
            lf_c = jnp.pad(caches['logf'][layer], ((0, 0), (0, 0), (0, LANES - N_HEADS)))
            lf_all = jnp.concatenate([lf_c, logf.reshape(b, l, LANES)], axis=1)
            padded = -(-(past + l) // LANES) * LANES
            lf_all = jnp.pad(lf_all, ((0, 0), (0, padded - past - l), (0, 0)))
            nck_all = _neg_cumsum_t(lf_all)[:, :N_HEADS]
            nck_c = nck_all[:, :, :past].reshape(b, hp, 2, past)
            nck_n = nck_all[:, :, past:past + l].reshape(b, hp, 2, l)
            o = _attn_sample(z3, caches['k'], caches['v'], layer, nck_c, nck_n)
            h0 = _s5_pack_state(caches['ssm_re'][layer], caches['ssm_im'][layer])
            buf = jnp.pad(caches['conv'][layer], ((0, 0), (SUBLANES - (CONV_W - 1), 0), (0, 0)))
        o = o.reshape(t, ATT_W)
        mats = _s5_matrices(wl['lam_re'], wl['lam_im'], wl['log_dt'], wl['b_re'], wl['b_im'],
                            wl['c_re'], wl['c_im'], wl['d_skip'])
        y, s_end = _s5(z3, u_col, mats, h0, min(64, n_chunks))
        y = y.reshape(t, g_s5 * S5_GROUP)
        s_re, s_im = _s5_unpack_state(s_end, g_s5)
        st['re'].append(s_re)
        st['im'].append(s_im)
        mix_in = _mix(o, y, z, gate_col, wl['w_att_proj'], wl['w_glu_v'], wl['w_glu_g'], tm, 512)
        x = _mm_norm(mix_in, wl['w_out'], x, wl['g_mix_post'], tm, 512)
        act, nbuf = _ffn_up(x, wl['g_ffn_pre'], wl['w_up'], wl['conv_w'], buf, l, tm_up, 512, long_seq)
        nbuf = nbuf.reshape(b, l // tm_up, SUBLANES, f)[:, -1]
        st['conv'].append(nbuf[:, SUBLANES - (CONV_W - 1):])
        x = _mm_norm(act, wl['w_down'], x, wl['g_ffn_post'], tm, 512)
    stacked = [jnp.stack(st[k]) for k in ('k', 'v', 'logf', 're', 'im', 'conv')]
    return x.reshape(b, l, d), stacked


def kernel(x_prompt, x_sample, cache_k, cache_v, cache_logf, state_ssm_re, state_ssm_im, state_conv,
           g_mix_pre, w_in, b_f, lam_re, lam_im, log_dt, b_re, b_im, c_re, c_im, d_skip,
           w_att_proj, w_glu_v, w_glu_g, w_out, g_mix_post, g_ffn_pre, w_up, conv_w, w_down, g_ffn_post):
    depth, d, _ = w_in.shape
    s5_w = d_skip.shape[1]
    f_lo, f_hi = 3 * ATT_W, 3 * ATT_W + N_HEADS
    scale = HEAD_DIM ** -0.5
    w_main = jnp.concatenate([w_in[:, :, :ATT_W] * scale, w_in[:, :, ATT_W:f_lo], w_in[:, :, f_hi:]], axis=2)
    weights = {
        'w_main': w_main.astype(BF16),
        'w_f': jnp.pad(w_in[:, :, f_lo:f_hi], ((0, 0), (0, 0), (0, LANES - N_HEADS))).astype(BF16),
        'b_f': jnp.pad(b_f, ((0, 0), (0, LANES - N_HEADS))).reshape(depth, 1, LANES),
        'g_mix_pre': g_mix_pre.reshape(depth, 1, d),
        'g_mix_post': g_mix_post.reshape(depth, 1, d),
        'g_ffn_pre': g_ffn_pre.reshape(depth, 1, d),
        'g_ffn_post': g_ffn_post.reshape(depth, 1, d),
        'lam_re': lam_re, 'lam_im': lam_im, 'log_dt': log_dt, 'b_re': b_re, 'b_im': b_im,
        'c_re': c_re, 'c_im': c_im, 'd_skip': d_skip,
        'w_att_proj': w_att_proj.astype(BF16),
        'w_glu_v': w_glu_v.astype(BF16),
        'w_glu_g': w_glu_g.astype(BF16),
        'w_out': w_out.astype(BF16),
        'w_up': w_up.astype(BF16),
        'conv_w': jnp.pad(conv_w, ((0, 0), (0, SUBLANES - CONV_W), (0, 0))),
        'w_down': w_down.astype(BF16),
    }
    assert s5_w == lam_re.shape[1] * S5_GROUP
    sb, sp = cache_k.shape[1], cache_k.shape[2]
    caches = {
        'k': cache_k.reshape(depth, sb, sp, ATT_W),
        'v': cache_v.reshape(depth, sb, sp, ATT_W),
        'logf': cache_logf, 'ssm_re': state_ssm_re, 'ssm_im': state_ssm_im, 'conv': state_conv,
    }
    y_prompt, p_st = _trunk(x_prompt, weights, None)
    y_sample, s_st = _trunk(x_sample, weights, caches)
    return (y_prompt, y_sample, *p_st, *s_st)
```

```python
import functools

import jax
import jax.numpy as jnp
from jax import lax
from jax.experimental import pallas as pl
from jax.experimental.pallas import tpu as pltpu

F32 = jnp.float32
BF16 = jnp.bfloat16
HIGHEST = lax.Precision.HIGHEST

RMS_EPS = 1e-6
N_HEADS = 16
HEAD_DIM = 64
ATT_W = N_HEADS * HEAD_DIM
S5_GROUP = 16
S5_STATE = 64
CONV_W = 3
LANES = 128
SUBLANES = 8
S5_CHUNK = 16
VMEM_LIMIT_BYTES = 56 * 1024 * 1024


def _params(*sem):
    return pltpu.CompilerParams(dimension_semantics=sem, vmem_limit_bytes=VMEM_LIMIT_BYTES)


def _gelu_tanh(x):
    return 0.5 * x * (1.0 + jnp.tanh(0.7978845608028654 * (x + 0.044715 * (x * x * x))))


def _sigmoid(x):
    return 1.0 / (1.0 + jnp.exp(-x))


def _log_sigmoid(x):
    return jnp.minimum(x, 0.0) - jnp.log1p(jnp.exp(-jnp.abs(x)))


def _rms_scale(x):
    return lax.rsqrt(jnp.mean(x * x, axis=-1, keepdims=True) + RMS_EPS)


def _norm_in_kernel(x_ref, g_ref, w_ref, wf_ref, bf_ref, z_ref, lf_ref, h_ref):
    @pl.when(pl.program_id(1) == 0)
    def _():
        x = x_ref[...]
        h = ((x * _rms_scale(x)) * g_ref[...]).astype(BF16)
        h_ref[...] = h
        fl = jnp.dot(h, wf_ref[...], preferred_element_type=F32) + bf_ref[...]
        lf_ref[...] = _log_sigmoid(fl)

    z_ref[...] = jnp.dot(h_ref[...], w_ref[...], preferred_element_type=F32)


def _norm_in(x, g, w, wf, bf, tm, tn):
    t, d = x.shape
    n = w.shape[1]
    return pl.pallas_call(
        _norm_in_kernel,
        grid=(t // tm, n // tn),
        in_specs=[
            pl.BlockSpec((tm, d), lambda i, j: (i, 0)),
            pl.BlockSpec((1, d), lambda i, j: (0, 0)),
            pl.BlockSpec((d, tn), lambda i, j: (0, j)),
            pl.BlockSpec((d, LANES), lambda i, j: (0, 0)),
            pl.BlockSpec((1, LANES), lambda i, j: (0, 0)),
        ],
        out_specs=[
            pl.BlockSpec((tm, tn), lambda i, j: (i, j)),
            pl.BlockSpec((tm, LANES), lambda i, j: (i, 0)),
        ],
        out_shape=[jax.ShapeDtypeStruct((t, n), F32), jax.ShapeDtypeStruct((t, LANES), F32)],
        scratch_shapes=[pltpu.VMEM((tm, d), BF16)],
        compiler_params=_params("arbitrary", "arbitrary"),
        name="norm_in",
    )(x, g, w, wf, bf)


CUMSUM_ROWS = 512


def _cumsum_rows_kernel(x_ref, o_ref, carry_ref):
    @pl.when(pl.program_id(1) == 0)
    def _():
        carry_ref[...] = jnp.zeros_like(carry_ref)

    n = x_ref.shape[0]
    row = lax.broadcasted_iota(jnp.int32, (n, n), 0)
    col = lax.broadcasted_iota(jnp.int32, (n, n), 1)
    lower = (col <= row).astype(F32)
    c = jnp.dot(lower, x_ref[...], precision=HIGHEST, preferred_element_type=F32) + carry_ref[...]
    o_ref[...] = c
    carry_ref[...] = c[n - 1:n, :]


def _cumsum_rows(x, blk):
    b, l, w = x.shape
    return pl.pallas_call(
        _cumsum_rows_kernel,
        grid=(b, l // blk),
        in_specs=[pl.BlockSpec((None, blk, w), lambda i, j: (i, j, 0))],
        out_specs=pl.BlockSpec((None, blk, w), lambda i, j: (i, j, 0)),
        out_shape=jax.ShapeDtypeStruct((b, l, w), F32),
        scratch_shapes=[pltpu.VMEM((1, w), F32)],
        compiler_params=_params("arbitrary", "arbitrary"),
        name="cumsum_rows",
    )(x)


def _head_masks():
    lane = lax.broadcasted_iota(jnp.int32, (1, LANES), 1)
    return lane < HEAD_DIM, lane >= HEAD_DIM


BIAS_PARTS = 3
V_ROWS = HEAD_DIM + 16


def _attn_kernel(q_ref, k_ref, v_ref, c_ref, o_ref, kt_ref, vto_ref,
                 kx_ref, vt_ref, m_ref, acc_ref, sa_ref, sb_ref, *, tq):
    hp = pl.program_id(1)
    i = pl.program_id(2)
    n_blk = k_ref.shape[0] // tq
    lane = lax.broadcasted_iota(jnp.int32, (1, LANES), 1)
    own = [(lane >= h * HEAD_DIM) & (lane < (h + 1) * HEAD_DIM) for h in range(2)]
    spare = [(1 - h) * HEAD_DIM for h in range(2)]

    @pl.when(i == 0)
    def _():
        er = lax.broadcasted_iota(jnp.int32, (LANES, LANES), 0)
        ec = lax.broadcasted_iota(jnp.int32, (LANES, LANES), 1)
        row = lax.broadcasted_iota(jnp.int32, (LANES, tq), 0)

        def prep(j, carry):
            r0 = pl.multiple_of(j * tq, tq)
            kblk = k_ref[pl.ds(r0, tq), :]
            nc = -c_ref[pl.ds(r0, tq), :]
            parts = []
            rem = nc
            for _ in range(BIAS_PARTS):
                piece = rem.astype(BF16)
                parts.append(piece)
                rem = rem - piece.astype(F32)
            vt = v_ref[pl.ds(r0, tq), :].T
            for h in range(2):
                bias = jnp.zeros((tq, LANES), F32)
                for n, piece in enumerate(parts):
                    sel = ((er == 2 * hp + h) & (ec == spare[h] + n)).astype(BF16)
                    bias = bias + jnp.dot(piece, sel, preferred_element_type=F32)
                kx_ref[h, pl.ds(r0, tq), :] = (jnp.where(own[h], kblk, 0.0) + bias).astype(BF16)
                vth = vt if h == 0 else pltpu.roll(vt, HEAD_DIM, 0)
                vt_ref[h, j] = jnp.where(row < HEAD_DIM, vth, 1.0)[:V_ROWS].astype(BF16)
            return carry

        lax.fori_loop(0, n_blk, prep, 0)

    own_rows = pl.ds(pl.multiple_of(i * tq, tq), tq)
    kt_ref[...] = k_ref[own_rows, :].T
    vto_ref[...] = v_ref[own_rows, :].T

    q = q_ref[...]
    qx = []
    for h in range(2):
        ones = (lane >= spare[h]) & (lane < spare[h] + BIAS_PARTS)
        qx.append((jnp.where(own[h], q, 0.0) + jnp.where(ones, 1.0, 0.0)).astype(BF16))
    m_ref[...] = jnp.full_like(m_ref, -jnp.inf)
    acc_ref[...] = jnp.zeros_like(acc_ref)
    nt_dims = (((1,), (1,)), ((), ()))

    def scores(kb, s_ref):
        start = pl.multiple_of(kb * tq, tq)
        for h in range(2):
            s_ref[h] = lax.dot_general(kx_ref[h, pl.ds(start, tq), :], qx[h], nt_dims,
                                       preferred_element_type=F32)

    def absorb(kb, s_ref, diagonal):
        sts = [s_ref[h] for h in range(2)]
        if diagonal:
            key = lax.broadcasted_iota(jnp.int32, (tq, tq), 0)
            qry = lax.broadcasted_iota(jnp.int32, (tq, tq), 1)
            sts = [jnp.where(key <= qry, st, -jnp.inf) for st in sts]
        m_old = [m_ref[h] for h in range(2)]
        m_new = [jnp.maximum(m_old[h], jnp.max(sts[h], axis=0, keepdims=True)) for h in range(2)]
        pts = [jnp.exp(sts[h] - m_new[h]).astype(BF16) for h in range(2)]
        pvs = [jnp.dot(vt_ref[h, kb], pts[h], preferred_element_type=F32) for h in range(2)]
        for h in range(2):
            acc_ref[h] = jnp.exp(m_old[h] - m_new[h]) * acc_ref[h] + pvs[h]
            m_ref[h] = m_new[h]

    scores(0, sa_ref)

    def pair(p, carry):
        kb = 2 * p
        scores(kb + 1, sb_ref)
        absorb(kb, sa_ref, False)
        scores(kb + 2, sa_ref)
        absorb(kb + 1, sb_ref, False)
        return carry

    lax.fori_loop(0, i // 2, pair, 0)

    @pl.when(i % 2 == 0)
    def _():
        absorb(i, sa_ref, True)

    @pl.when(i % 2 == 1)
    def _():
        scores(i, sb_ref)
        absorb(i - 1, sa_ref, False)
        absorb(i, sb_ref, True)

    outs = []
    for h in range(2):
        a = acc_ref[h]
        outs.append(a[:HEAD_DIM] * (1.0 / a[HEAD_DIM:HEAD_DIM + 1]))
    o_ref[...] = jnp.concatenate(outs, axis=0).T.astype(o_ref.dtype)


def _attn_prompt(z, c, tq):
    b, l, _ = z.shape
    hp = ATT_W // LANES
    return pl.pallas_call(
        functools.partial(_attn_kernel, tq=tq),
        grid=(b, hp, l // tq),
        in_specs=[
            pl.BlockSpec((None, tq, LANES), lambda bi, h, i: (bi, i, h)),
            pl.BlockSpec((None, l, LANES), lambda bi, h, i: (bi, 0, hp + h)),
            pl.BlockSpec((None, l, LANES), lambda bi, h, i: (bi, 0, 2 * hp + h)),
            pl.BlockSpec((None, l, LANES), lambda bi, h, i: (bi, 0, 0)),
        ],
        out_specs=[
            pl.BlockSpec((None, tq, LANES), lambda bi, h, i: (bi, i, h)),
            pl.BlockSpec((None, LANES, tq), lambda bi, h, i: (bi, h, i)),
            pl.BlockSpec((None, LANES, tq), lambda bi, h, i: (bi, h, i)),
        ],
        out_shape=[
            jax.ShapeDtypeStruct((b, l, ATT_W), BF16),
            jax.ShapeDtypeStruct((b, ATT_W, l), F32),
            jax.ShapeDtypeStruct((b, ATT_W, l), F32),
        ],
        scratch_shapes=[
            pltpu.VMEM((2, l, LANES), BF16),
            pltpu.VMEM((2, l // tq, V_ROWS, tq), BF16),
            pltpu.VMEM((2, 1, tq), F32),
            pltpu.VMEM((2, V_ROWS, tq), F32),
            pltpu.VMEM((2, tq, tq), F32),
            pltpu.VMEM((2, tq, tq), F32),
        ],
        compiler_params=_params("arbitrary", "arbitrary", "arbitrary"),
        name="attn_prompt",
    )(z, z, z, c)


def _attn_cache_kernel(q_ref, kn_ref, vn_ref, kc_ref, vc_ref, nckc_ref, nckn_ref, o_ref):
    q = q_ref[...]
    lq = q.shape[0]
    masks = _head_masks()
    kc = kc_ref[...].astype(BF16)
    vc = vc_ref[...].astype(BF16)
    kn = kn_ref[...].astype(BF16)
    vn = vn_ref[...].astype(BF16)
    nt_dims = (((1,), (1,)), ((), ()))
    r = lax.broadcasted_iota(jnp.int32, (lq, lq), 0)
    c = lax.broadcasted_iota(jnp.int32, (lq, lq), 1)
    outs = []
    for h in range(2):
        qh = jnp.where(masks[h], q, 0.0).astype(BF16)
        sc = lax.dot_general(qh, kc, nt_dims, preferred_element_type=F32) + nckc_ref[pl.ds(h, 1), :]
        sn = lax.dot_general(qh, kn, nt_dims, preferred_element_type=F32) + nckn_ref[pl.ds(h, 1), :]
        sn = jnp.where(c <= r, sn, -jnp.inf)
        m = jnp.maximum(jnp.max(sc, axis=-1, keepdims=True), jnp.max(sn, axis=-1, keepdims=True))
        pc = jnp.exp(sc - m)
        pn = jnp.exp(sn - m)
        den = jnp.sum(pc, axis=-1, keepdims=True) + jnp.sum(pn, axis=-1, keepdims=True)
        acc = (jnp.dot(pc.astype(BF16), vc, preferred_element_type=F32)
               + jnp.dot(pn.astype(BF16), vn, preferred_element_type=F32))
        outs.append(acc / den)
    o_ref[...] = jnp.where(masks[0], outs[0], outs[1]).astype(o_ref.dtype)


def _attn_sample(z, cache_k, cache_v, layer, nck_c, nck_n):
    b, lq, _ = z.shape
    p = cache_k.shape[2]
    hp = ATT_W // LANES
    return pl.pallas_call(
        _attn_cache_kernel,
        grid=(b, hp),
        in_specs=[
            pl.BlockSpec((None, lq, LANES), lambda bi, h: (bi, 0, h)),
            pl.BlockSpec((None, lq, LANES), lambda bi, h: (bi, 0, hp + h)),
            pl.BlockSpec((None, lq, LANES), lambda bi, h: (bi, 0, 2 * hp + h)),
            pl.BlockSpec((None, None, p, LANES), lambda bi, h: (layer, bi, 0, h)),
            pl.BlockSpec((None, None, p, LANES), lambda bi, h: (layer, bi, 0, h)),
            pl.BlockSpec((None, None, 2, p), lambda bi, h: (bi, h, 0, 0)),
            pl.BlockSpec((None, None, 2, lq), lambda bi, h: (bi, h, 0, 0)),
        ],
        out_specs=pl.BlockSpec((None, lq, LANES), lambda bi, h: (bi, 0, h)),
        out_shape=jax.ShapeDtypeStruct((b, lq, ATT_W), BF16),
        compiler_params=_params("arbitrary", "arbitrary"),
        name="attn_sample",
    )(z, z, z, cache_k, cache_v, nck_c, nck_n)


S5_TILE_GROUPS = LANES // S5_GROUP
S5_TILE_STATE = 2 * S5_TILE_GROUPS * S5_STATE


def _s5_kernel(u_ref, bw_ref, bd_ref, bv_ref, are_ref, aim_ref, d_ref, h0_ref,
               y_ref, send_ref, s_ref, e_ref, sp_ref, *, nb, cb):
    t = pl.program_id(1)
    tc = S5_CHUNK
    rows = nb * cb

    @pl.when(t == 0)
    def _():
        s_ref[...] = h0_ref[...]

    def tokens(j):
        if cb == 1:
            return u_ref[:, j, :]
        return u_ref[:, pl.ds(j, cb, stride=tc), :].reshape(rows, LANES)

    us = [tokens(j) for j in range(tc)]
    ucat = jnp.concatenate([u.astype(BF16) for u in us], axis=1)
    e = jnp.dot(ucat, bw_ref[...], preferred_element_type=F32)
    e_ref[...] = e.reshape(nb, cb, S5_TILE_STATE)

    a_re = are_ref[...]
    a_im = aim_ref[...]

    def advance(c, carry):
        for b in range(nb):
            s = s_ref[b]
            sp_ref[b, pl.ds(c, 1), :] = s
            s_ref[b] = a_re * s + a_im * pltpu.roll(s, S5_TILE_STATE // 2, 1) + e_ref[b, pl.ds(c, 1), :]
        return carry

    lax.fori_loop(0, cb, advance, 0)

    sp = sp_ref[...].reshape(rows, S5_TILE_STATE).astype(BF16)
    d = d_ref[...]
    y_state = jnp.dot(sp, bv_ref[...], preferred_element_type=F32)
    pair = 2 * LANES
    for k in range(tc // 2):
        kk = pair * (k + 1)
        yk = jnp.dot(ucat[:, :kk], bd_ref[:kk, k * pair:(k + 1) * pair], preferred_element_type=F32)
        yk = yk + y_state[:, k * pair:(k + 1) * pair]
        for jj in range(2):
            jo = 2 * k + jj
            y = yk[:, jj * LANES:(jj + 1) * LANES] + us[jo] * d
            if cb == 1:
                y_ref[:, jo, :] = y
            else:
                y_ref[:, pl.ds(jo, cb, stride=tc), :] = y.reshape(nb, cb, LANES)

    @pl.when(t == pl.num_programs(1) - 1)
    def _():
        send_ref[...] = s_ref[...]


def _s5(z3, u_col, mats, h0, cb):
    bw, bd, bv, a_re, a_im, d = mats
    nb, l, _ = z3.shape
    tiles = bw.shape[0]
    tc = S5_CHUNK
    blk = tc * cb
    col0 = u_col // LANES
    return pl.pallas_call(
        functools.partial(_s5_kernel, nb=nb, cb=cb),
        grid=(tiles, l // blk),
        in_specs=[
            pl.BlockSpec((nb, blk, LANES), lambda g, t: (0, t, col0 + g)),
            pl.BlockSpec((None, tc * LANES, S5_TILE_STATE), lambda g, t: (g, 0, 0)),
            pl.BlockSpec((None, tc * LANES, tc * LANES), lambda g, t: (g, 0, 0)),
            pl.BlockSpec((None, S5_TILE_STATE, tc * LANES), lambda g, t: (g, 0, 0)),
            pl.BlockSpec((None, 1, S5_TILE_STATE), lambda g, t: (g, 0, 0)),
            pl.BlockSpec((None, 1, S5_TILE_STATE), lambda g, t: (g, 0, 0)),
            pl.BlockSpec((None, 1, LANES), lambda g, t: (g, 0, 0)),
            pl.BlockSpec((nb, None, 1, S5_TILE_STATE), lambda g, t: (0, g, 0, 0)),
        ],
        out_specs=[
            pl.BlockSpec((nb, blk, LANES), lambda g, t: (0, t, g)),
            pl.BlockSpec((nb, None, 1, S5_TILE_STATE), lambda g, t: (0, g, 0, 0)),
        ],
        out_shape=[jax.ShapeDtypeStruct((nb, l, tiles * LANES), F32), jax.ShapeDtypeStruct(h0.shape, F32)],
        scratch_shapes=[
            pltpu.VMEM((nb, 1, S5_TILE_STATE), F32),
            pltpu.VMEM((nb, cb, S5_TILE_STATE), F32),
            pltpu.VMEM((nb, cb, S5_TILE_STATE), F32),
        ],
        compiler_params=_params("arbitrary", "arbitrary"),
        name="s5",
    )(z3, bw, bd, bv, a_re, a_im, d, h0)


def _s5_matrices(lam_re, lam_im, log_dt, b_re, b_im, c_re, c_im, d_skip):
    g, p = lam_re.shape
    tc = S5_CHUNK
    dt = jnp.exp(log_dt)[:, None]
    k = jnp.arange(tc + 1, dtype=F32)[:, None, None]
    mag = jnp.exp(lam_re * dt * k)
    pw_re = mag * jnp.cos(lam_im * dt * k)
    pw_im = mag * jnp.sin(lam_im * dt * k)
    den = lam_re * lam_re + lam_im * lam_im
    nr = pw_re[1] - 1.0
    f_re = (nr * lam_re + pw_im[1] * lam_im) / den
    f_im = (pw_im[1] * lam_re - nr * lam_im) / den
    bb_re = f_re[..., None] * b_re - f_im[..., None] * b_im
    bb_im = f_re[..., None] * b_im + f_im[..., None] * b_re
    ein = functools.partial(jnp.einsum, precision=HIGHEST)
    cp_re = c_re[None] * pw_re[:tc, :, None, :] - c_im[None] * pw_im[:tc, :, None, :]
    cp_im = c_re[None] * pw_im[:tc, :, None, :] + c_im[None] * pw_re[:tc, :, None, :]
    taps = ein('tghp,gpk->tghk', cp_re, bb_re) - ein('tghp,gpk->tghk', cp_im, bb_im)
    k_rev = (tc - 1) - jnp.arange(tc, dtype=F32)[:, None, None]
    mag_rev = jnp.exp(lam_re * dt * k_rev)
    rp_re = (mag_rev * jnp.cos(lam_im * dt * k_rev))[:, :, :, None]
    rp_im = (mag_rev * jnp.sin(lam_im * dt * k_rev))[:, :, :, None]
    w_re = rp_re * bb_re[None] - rp_im * bb_im[None]
    w_im = rp_re * bb_im[None] + rp_im * bb_re[None]
    z_re = c_re[None] * pw_re[1:, :, None, :] - c_im[None] * pw_im[1:, :, None, :]
    z_im = c_re[None] * pw_im[1:, :, None, :] + c_im[None] * pw_re[1:, :, None, :]
    gl = S5_TILE_GROUPS
    tiles = g // gl
    h = S5_GROUP
    eye = jnp.eye(gl, dtype=F32)
    tp = taps.reshape(tc, tiles, gl, h, h).transpose(1, 0, 2, 4, 3)
    bd = tp[:, :, :, :, None, :] * eye[None, None, :, None, :, None]
    bd = bd.reshape(tiles, tc, LANES, LANES).astype(BF16)
    lag = jnp.arange(tc)[None, :] - jnp.arange(tc)[:, None]
    bd = jnp.where((lag >= 0)[None, :, :, None, None], bd[:, jnp.clip(lag, 0, tc - 1)], 0)
    bd = bd.transpose(0, 1, 3, 2, 4).reshape(tiles, tc * LANES, tc * LANES)
    ws = jnp.stack([w_re, w_im]).reshape(2, tc, tiles, gl, p, h).transpose(2, 1, 3, 5, 0, 4)
    bw = ws[:, :, :, :, :, None, :] * eye[None, None, :, None, None, :, None]
    bw = bw.reshape(tiles, tc * LANES, S5_TILE_STATE)
    zs = jnp.stack([z_re, -z_im]).reshape(2, tc, tiles, gl, h, p).transpose(2, 0, 3, 5, 1, 4)
    bv = zs[:, :, :, :, :, None, :] * eye[None, None, :, None, None, :, None]
    bv = bv.reshape(tiles, S5_TILE_STATE, tc * LANES)
    ar = pw_re[tc].reshape(tiles, gl * p)
    ai = pw_im[tc].reshape(tiles, gl * p)
    a_re = jnp.concatenate([ar, ar], axis=-1)[:, None, :]
    a_im = jnp.concatenate([-ai, ai], axis=-1)[:, None, :]
    d = d_skip.reshape(tiles, 1, LANES)
    return bw.astype(BF16), bd.astype(BF16), bv.astype(BF16), a_re, a_im, d


def _s5_pack_state(re, im):
    b, g, p = re.shape
    tiles = g // S5_TILE_GROUPS
    return jnp.concatenate([re.reshape(b, tiles, 1, -1), im.reshape(b, tiles, 1, -1)], axis=-1)


def _s5_unpack_state(s, g):
    b = s.shape[0]
    half = S5_TILE_STATE // 2
    return s[..., :half].reshape(b, g, S5_STATE), s[..., half:].reshape(b, g, S5_STATE)


def _mix_kernel(o_ref, y_ref, ga_ref, gb_ref, wa_ref, wv_ref, wg_ref, out_ref, g5_ref):
    @pl.when(pl.program_id(1) == 0)
    def _():
        g5_ref[...] = _gelu_tanh(y_ref[...].astype(F32)).astype(BF16)

    ya = jnp.dot(o_ref[...], wa_ref[...], preferred_element_type=F32)
    g5 = g5_ref[...]
    yb = (jnp.dot(g5, wv_ref[...], preferred_element_type=F32)
          * _sigmoid(jnp.dot(g5, wg_ref[...], preferred_element_type=F32)))
    out = _sigmoid(ga_ref[...]) * ya + _sigmoid(gb_ref[...]) * yb
    out_ref[...] = out.astype(out_ref.dtype)


def _mix(o, y, z, gate_col, wa, wv, wg, tm, tn):
    t, kdim = o.shape
    n = wa.shape[1]
    ga0 = gate_col // tn
    gb0 = (gate_col + n) // tn
    return pl.pallas_call(
        _mix_kernel,
        grid=(t // tm, n // tn),
        in_specs=[
            pl.BlockSpec((tm, kdim), lambda i, j: (i, 0)),
            pl.BlockSpec((tm, kdim), lambda i, j: (i, 0)),
            pl.BlockSpec((tm, tn), lambda i, j: (i, ga0 + j)),
            pl.BlockSpec((tm, tn), lambda i, j: (i, gb0 + j)),
            pl.BlockSpec((kdim, tn), lambda i, j: (0, j)),
            pl.BlockSpec((kdim, tn), lambda i, j: (0, j)),
            pl.BlockSpec((kdim, tn), lambda i, j: (0, j)),
        ],
        out_specs=pl.BlockSpec((tm, tn), lambda i, j: (i, j)),
        out_shape=jax.ShapeDtypeStruct((t, n), BF16),
        scratch_shapes=[pltpu.VMEM((tm, kdim), BF16)],
        compiler_params=_params("arbitrary", "arbitrary"),
        name="mix",
    )(o, y, z, z, wa, wv, wg)


def _mm_norm_kernel(a_ref, w_ref, x_ref, g_ref, out_ref):
    k = pl.program_id(1)
    part = jnp.dot(a_ref[...], w_ref[...], preferred_element_type=F32)

    @pl.when(k == 0)
    def _():
        out_ref[...] = part

    @pl.when(k != 0)
    def _():
        out_ref[...] += part

    @pl.when(k == pl.num_programs(1) - 1)
    def _():
        m = out_ref[...]
        out_ref[...] = x_ref[...] + (m * _rms_scale(m)) * g_ref[...]


def _mm_norm(a, w, x, g, tm, tk):
    t, kdim = a.shape
    n = w.shape[1]
    return pl.pallas_call(
        _mm_norm_kernel,
        grid=(t // tm, kdim // tk),
        in_specs=[
            pl.BlockSpec((tm, tk), lambda i, k: (i, k)),
            pl.BlockSpec((tk, n), lambda i, k: (k, 0)),
            pl.BlockSpec((tm, n), lambda i, k: (i, 0)),
            pl.BlockSpec((1, n), lambda i, k: (0, 0)),
        ],
        out_specs=pl.BlockSpec((tm, n), lambda i, k: (i, 0)),
        out_shape=jax.ShapeDtypeStruct((t, n), F32),
        compiler_params=_params("arbitrary", "arbitrary"),
        name="mm_norm",
    )(a, w, x, g)


def _ffn_up_kernel(x_ref, g_ref, wa_ref, wb_ref, cw_ref, buf_ref, act_ref, nbuf_ref, h_ref, tail_ref,
                   *, rows_outer, blocks_per_seq):
    if rows_outer:
        i, j = pl.program_id(0), pl.program_id(1)
        fresh_rows = j == 0
    else:
        j, i = pl.program_id(0), pl.program_id(1)
        fresh_rows = True

    def normalise():
        x = x_ref[...]
        h_ref[...] = ((x * _rms_scale(x)) * g_ref[...]).astype(BF16)

    if rows_outer:
        pl.when(fresh_rows)(normalise)
    else:
        normalise()

    if blocks_per_seq > 1:
        @pl.when((i == 0) & (j == 0))
        def _():
            tail_ref[...] = jnp.zeros_like(tail_ref)

    h = h_ref[...]
    a = jnp.dot(h, wa_ref[...], preferred_element_type=F32)
    b = jnp.dot(h, wb_ref[...], preferred_element_type=F32)
    tm = a.shape[0]
    if blocks_per_seq == 1:
        prev = buf_ref[...]
    else:
        prev = jnp.where(i % blocks_per_seq == 0, buf_ref[...], tail_ref[j])
    last = a[tm - SUBLANES:, :]
    tail_ref[j] = last
    nbuf_ref[...] = last
    row = lax.broadcasted_iota(jnp.int32, a.shape, 0)
    p1 = prev[SUBLANES - 1:SUBLANES, :]
    p2 = prev[SUBLANES - 2:SUBLANES - 1, :]
    a1 = jnp.where(row == 0, p1, pltpu.roll(a, 1, 0))
    a2 = jnp.where(row == 0, p2, jnp.where(row == 1, p1, pltpu.roll(a, 2, 0)))
    cw = cw_ref[...]
    conv = a2 * cw[0:1, :] + a1 * cw[1:2, :] + a * cw[2:3, :]
    act_ref[...] = (_gelu_tanh(conv) * b).astype(act_ref.dtype)


def _ffn_up(x, g, w_up, cw, buf, seq_len, tm, tn, rows_outer):
    t, d = x.shape
    f = w_up.shape[1] // 2
    nj = f // tn
    bps = seq_len // tm
    if rows_outer:
        grid = (t // tm, nj)
        ij = lambda a, b: (a, b)
    else:
        grid = (nj, t // tm)
        ij = lambda a, b: (b, a)

    def spec(shape, fn):
        return pl.BlockSpec(shape, lambda a, b: fn(*ij(a, b)))

    return pl.pallas_call(
        functools.partial(_ffn_up_kernel, rows_outer=rows_outer, blocks_per_seq=bps),
        grid=grid,
        in_specs=[
            spec((tm, d), lambda i, j: (i, 0)),
            spec((1, d), lambda i, j: (0, 0)),
            spec((d, tn), lambda i, j: (0, j)),
            spec((d, tn), lambda i, j: (0, nj + j)),
            spec((SUBLANES, tn), lambda i, j: (0, j)),
            spec((None, SUBLANES, tn), lambda i, j: (i // bps, 0, j)),
        ],
        out_specs=[
            spec((tm, tn), lambda i, j: (i, j)),
            spec((None, SUBLANES, tn), lambda i, j: (i, 0, j)),
        ],
        out_shape=[jax.ShapeDtypeStruct((t, f), BF16), jax.ShapeDtypeStruct((t // tm, SUBLANES, f), F32)],
        scratch_shapes=[pltpu.VMEM((tm, d), BF16), pltpu.VMEM((nj, SUBLANES, tn), F32)],
        compiler_params=_params("arbitrary", "arbitrary"),
        name="ffn_up",
    )(x, g, w_up, w_up, cw, buf)


def _pick(t, pref):
    return pref if t % pref == 0 else t


def _trunk(x, weights, caches):
    b, l, d = x.shape
    t = b * l
    depth = weights['w_main'].shape[0]
    n_main = weights['w_main'].shape[2]
    gate_col = n_main - 2 * d
    u_col = 3 * ATT_W
    g_s5 = weights['lam_re'].shape[1]
    f = weights['w_down'].shape[1]
    n_chunks = l // S5_CHUNK
    tm = _pick(t, 1024)
    long_seq = l % tm == 0
    tm_up = tm if long_seq else l
    hp = ATT_W // LANES
    x = x.reshape(t, d)
    st = {k: [] for k in ('k', 'v', 'logf', 're', 'im', 'conv')}
    for layer in range(depth):
        wl = {k: v[layer] for k, v in weights.items()}
        z, logf = _norm_in(x, wl['g_mix_pre'], wl['w_main'], wl['w_f'], wl['b_f'], tm, 512)
        st['logf'].append(logf[:, :N_HEADS].reshape(b, l, N_HEADS))
        z3 = z.reshape(b, l, n_main)
        if caches is None:
            c = _cumsum_rows(logf.reshape(b, l, LANES), CUMSUM_ROWS)
            o, kt, vt = _attn_prompt(z3, c, 512)
            st['k'].append(kt.reshape(b, N_HEADS, HEAD_DIM, l))
            st['v'].append(vt.reshape(b, N_HEADS, HEAD_DIM, l))
            h0 = jnp.zeros((b, g_s5 // S5_TILE_GROUPS, 1, S5_TILE_STATE), F32)
            buf = jnp.zeros((b, SUBLANES, f), F32)
        else:
            st['k'].append(z[:, ATT_W:2 * ATT_W].reshape(b, l, N_HEADS, HEAD_DIM))
            st['v'].append(z[:, 2 * ATT_W:3 * ATT_W].reshape(b, l, N_HEADS, HEAD_DIM))
            past = caches['logf'].shape[2]
            lf_c = jnp.pad(caches['logf'][layer], ((0, 0), (0, 0), (0, LANES - N_HEADS)))
            lf_all = jnp.concatenate([lf_c, logf.reshape(b, l, LANES)], axis=1)
            padded = -(-(past + l) // CUMSUM_ROWS) * CUMSUM_ROWS
            lf_all = jnp.pad(lf_all, ((0, 0), (0, padded - past - l), (0, 0)))
            nck_all = -_cumsum_rows(lf_all, CUMSUM_ROWS)[:, :, :N_HEADS].transpose(0, 2, 1)
            nck_c = nck_all[:, :, :past].reshape(b, hp, 2, past)
            nck_n = nck_all[:, :, past:past + l].reshape(b, hp, 2, l)
            o = _attn_sample(z3, caches['k'], caches['v'], layer, nck_c, nck_n)
            h0 = _s5_pack_state(caches['ssm_re'][layer], caches['ssm_im'][layer])
            buf = jnp.pad(caches['conv'][layer], ((0, 0), (SUBLANES - (CONV_W - 1), 0), (0, 0)))
        o = o.reshape(t, ATT_W)
        mats = _s5_matrices(wl['lam_re'], wl['lam_im'], wl['log_dt'], wl['b_re'], wl['b_im'],
                            wl['c_re'], wl['c_im'], wl['d_skip'])
        y, s_end = _s5(z3, u_col, mats, h0, min(64, n_chunks))
        y = y.reshape(t, g_s5 * S5_GROUP)
        s_re, s_im = _s5_unpack_state(s_end, g_s5)
        st['re'].append(s_re)
        st['im'].append(s_im)
        mix_in = _mix(o, y, z, gate_col, wl['w_att_proj'], wl['w_glu_v'], wl['w_glu_g'], tm, 512)
        x = _mm_norm(mix_in, wl['w_out'], x, wl['g_mix_post'], tm, 512)
        act, nbuf = _ffn_up(x, wl['g_ffn_pre'], wl['w_up'], wl['conv_w'], buf, l, tm_up, 512, long_seq)
        nbuf = nbuf.reshape(b, l // tm_up, SUBLANES, f)[:, -1]
        st['conv'].append(nbuf[:, SUBLANES - (CONV_W - 1):])
        x = _mm_norm(act, wl['w_down'], x, wl['g_ffn_post'], tm, 512)
    stacked = [jnp.stack(st[k]) for k in ('k', 'v', 'logf', 're', 'im', 'conv')]
    if caches is None:
        stacked[0] = stacked[0].transpose(0, 1, 4, 2, 3)
        stacked[1] = stacked[1].transpose(0, 1, 4, 2, 3)
    return x.reshape(b, l, d), stacked


def kernel(x_prompt, x_sample, cache_k, cache_v, cache_logf, state_ssm_re, state_ssm_im, state_conv,
           g_mix_pre, w_in, b_f, lam_re, lam_im, log_dt, b_re, b_im, c_re, c_im, d_skip,
           w_att_proj, w_glu_v, w_glu_g, w_out, g_mix_post, g_ffn_pre, w_up, conv_w, w_down, g_ffn_post):
    depth, d, _ = w_in.shape
    s5_w = d_skip.shape[1]
    f_lo, f_hi = 3 * ATT_W, 3 * ATT_W + N_HEADS
    scale = HEAD_DIM ** -0.5
    w_main = jnp.concatenate([w_in[:, :, :ATT_W] * scale, w_in[:, :, ATT_W:f_lo], w_in[:, :, f_hi:]], axis=2)
    weights = {
        'w_main': w_main.astype(BF16),
        'w_f': jnp.pad(w_in[:, :, f_lo:f_hi], ((0, 0), (0, 0), (0, LANES - N_HEADS))).astype(BF16),
        'b_f': jnp.pad(b_f, ((0, 0), (0, LANES - N_HEADS))).reshape(depth, 1, LANES),
        'g_mix_pre': g_mix_pre.reshape(depth, 1, d),
        'g_mix_post': g_mix_post.reshape(depth, 1, d),
        'g_ffn_pre': g_ffn_pre.reshape(depth, 1, d),
        'g_ffn_post': g_ffn_post.reshape(depth, 1, d),
        'lam_re': lam_re, 'lam_im': lam_im, 'log_dt': log_dt, 'b_re': b_re, 'b_im': b_im,
        'c_re': c_re, 'c_im': c_im, 'd_skip': d_skip,
        'w_att_proj': w_att_proj.astype(BF16),
        'w_glu_v': w_glu_v.astype(BF16),
        'w_glu_g': w_glu_g.astype(BF16),
        'w_out': w_out.astype(BF16),
        'w_up': w_up.astype(BF16),
        'conv_w': jnp.pad(conv_w, ((0, 0), (0, SUBLANES - CONV_W), (0, 0))),
        'w_down': w_down.astype(BF16),
    }
    assert s5_w == lam_re.shape[1] * S5_GROUP
    sb, sp = cache_k.shape[1], cache_k.shape[2]
    caches = {
        'k': cache_k.reshape(depth, sb, sp, ATT_W),
        'v': cache_v.reshape(depth, sb, sp, ATT_W),
        'logf': cache_logf, 'ssm_re': state_ssm_re, 'ssm_im': state_ssm_im, 'conv': state_conv,
    }
    y_prompt, p_st = _trunk(x_prompt, weights, None)
    y_sample, s_st = _trunk(x_sample, weights, caches)
    return (y_prompt, y_sample, *p_st, *s_st)
```

```python
import functools

import jax
import jax.numpy as jnp
from jax import lax
from jax.experimental import pallas as pl
from jax.experimental.pallas import tpu as pltpu

F32 = jnp.float32
BF16 = jnp.bfloat16
HIGHEST = lax.Precision.HIGHEST

RMS_EPS = 1e-6
N_HEADS = 16
HEAD_DIM = 64
ATT_W = N_HEADS * HEAD_DIM
S5_GROUP = 16
S5_STATE = 64
CONV_W = 3
LANES = 128
SUBLANES = 8
S5_CHUNK = 16
VMEM_LIMIT_BYTES = 56 * 1024 * 1024


def _params(*sem):
    return pltpu.CompilerParams(dimension_semantics=sem, vmem_limit_bytes=VMEM_LIMIT_BYTES)


def _gelu_tanh(x):
    return 0.5 * x * (1.0 + jnp.tanh(0.7978845608028654 * (x + 0.044715 * (x * x * x))))


def _sigmoid(x):
    return 1.0 / (1.0 + jnp.exp(-x))


def _log_sigmoid(x):
    return jnp.minimum(x, 0.0) - jnp.log1p(jnp.exp(-jnp.abs(x)))


def _rms_scale(x):
    return lax.rsqrt(jnp.mean(x * x, axis=-1, keepdims=True) + RMS_EPS)


def _norm_in_kernel(x_ref, g_ref, w_ref, wf_ref, bf_ref, z_ref, lf_ref, h_ref):
    @pl.when(pl.program_id(1) == 0)
    def _():
        x = x_ref[...]
        h = ((x * _rms_scale(x)) * g_ref[...]).astype(BF16)
        h_ref[...] = h
        fl = jnp.dot(h, wf_ref[...], preferred_element_type=F32) + bf_ref[...]
        lf_ref[...] = _log_sigmoid(fl)

    z_ref[...] = jnp.dot(h_ref[...], w_ref[...], preferred_element_type=F32)


def _norm_in(x, g, w, wf, bf, tm, tn):
    t, d = x.shape
    n = w.shape[1]
    return pl.pallas_call(
        _norm_in_kernel,
        grid=(t // tm, n // tn),
        in_specs=[
            pl.BlockSpec((tm, d), lambda i, j: (i, 0)),
            pl.BlockSpec((1, d), lambda i, j: (0, 0)),
            pl.BlockSpec((d, tn), lambda i, j: (0, j)),
            pl.BlockSpec((d, LANES), lambda i, j: (0, 0)),
            pl.BlockSpec((1, LANES), lambda i, j: (0, 0)),
        ],
        out_specs=[
            pl.BlockSpec((tm, tn), lambda i, j: (i, j)),
            pl.BlockSpec((tm, LANES), lambda i, j: (i, 0)),
        ],
        out_shape=[jax.ShapeDtypeStruct((t, n), F32), jax.ShapeDtypeStruct((t, LANES), F32)],
        scratch_shapes=[pltpu.VMEM((tm, d), BF16)],
        compiler_params=_params("arbitrary", "arbitrary"),
        name="norm_in",
    )(x, g, w, wf, bf)


CUMSUM_ROWS = 512


def _cumsum_rows_kernel(x_ref, o_ref, carry_ref):
    @pl.when(pl.program_id(1) == 0)
    def _():
        carry_ref[...] = jnp.zeros_like(carry_ref)

    n = x_ref.shape[0]
    row = lax.broadcasted_iota(jnp.int32, (n, n), 0)
    col = lax.broadcasted_iota(jnp.int32, (n, n), 1)
    lower = (col <= row).astype(F32)
    c = jnp.dot(lower, x_ref[...], precision=HIGHEST, preferred_element_type=F32) + carry_ref[...]
    o_ref[...] = c
    carry_ref[...] = c[n - 1:n, :]


def _cumsum_rows(x, blk):
    b, l, w = x.shape
    return pl.pallas_call(
        _cumsum_rows_kernel,
        grid=(b, l // blk),
        in_specs=[pl.BlockSpec((None, blk, w), lambda i, j: (i, j, 0))],
        out_specs=pl.BlockSpec((None, blk, w), lambda i, j: (i, j, 0)),
        out_shape=jax.ShapeDtypeStruct((b, l, w), F32),
        scratch_shapes=[pltpu.VMEM((1, w), F32)],
        compiler_params=_params("arbitrary", "arbitrary"),
        name="cumsum_rows",
    )(x)


def _head_masks():
    lane = lax.broadcasted_iota(jnp.int32, (1, LANES), 1)
    return lane < HEAD_DIM, lane >= HEAD_DIM


BIAS_PARTS = 3
V_ROWS = HEAD_DIM + 16


def _attn_kernel(q_ref, k_ref, v_ref, c_ref, o_ref, kt_ref, vto_ref,
                 kx_ref, vt_ref, m_ref, acc_ref, sa_ref, sb_ref, *, tq):
    hp = pl.program_id(1)
    i = pl.program_id(2)
    n_blk = k_ref.shape[0] // tq
    lane = lax.broadcasted_iota(jnp.int32, (1, LANES), 1)
    own = [(lane >= h * HEAD_DIM) & (lane < (h + 1) * HEAD_DIM) for h in range(2)]
    spare = [(1 - h) * HEAD_DIM for h in range(2)]

    @pl.when(i == 0)
    def _():
        er = lax.broadcasted_iota(jnp.int32, (LANES, LANES), 0)
        ec = lax.broadcasted_iota(jnp.int32, (LANES, LANES), 1)
        row = lax.broadcasted_iota(jnp.int32, (LANES, tq), 0)

        def prep(j, carry):
            r0 = pl.multiple_of(j * tq, tq)
            kblk = k_ref[pl.ds(r0, tq), :]
            nc = -c_ref[pl.ds(r0, tq), :]
            parts = []
            rem = nc
            for _ in range(BIAS_PARTS):
                piece = rem.astype(BF16)
                parts.append(piece)
                rem = rem - piece.astype(F32)
            vt = v_ref[pl.ds(r0, tq), :].T
            for h in range(2):
                bias = jnp.zeros((tq, LANES), F32)
                for n, piece in enumerate(parts):
                    sel = ((er == 2 * hp + h) & (ec == spare[h] + n)).astype(BF16)
                    bias = bias + jnp.dot(piece, sel, preferred_element_type=F32)
                kx_ref[h, pl.ds(r0, tq), :] = (jnp.where(own[h], kblk, 0.0) + bias).astype(BF16)
                vth = vt if h == 0 else pltpu.roll(vt, HEAD_DIM, 0)
                vt_ref[h, j] = jnp.where(row < HEAD_DIM, vth, 1.0)[:V_ROWS].astype(BF16)
            return carry

        lax.fori_loop(0, n_blk, prep, 0)

    own_rows = pl.ds(pl.multiple_of(i * tq, tq), tq)
    kt_ref[...] = k_ref[own_rows, :].T
    vto_ref[...] = v_ref[own_rows, :].T

    q = q_ref[...]
    qx = []
    for h in range(2):
        ones = (lane >= spare[h]) & (lane < spare[h] + BIAS_PARTS)
        qx.append((jnp.where(own[h], q, 0.0) + jnp.where(ones, 1.0, 0.0)).astype(BF16))
    m_ref[...] = jnp.full_like(m_ref, -jnp.inf)
    acc_ref[...] = jnp.zeros_like(acc_ref)
    nt_dims = (((1,), (1,)), ((), ()))

    def scores(kb, s_ref):
        start = pl.multiple_of(kb * tq, tq)
        for h in range(2):
            s_ref[h] = lax.dot_general(kx_ref[h, pl.ds(start, tq), :], qx[h], nt_dims,
                                       preferred_element_type=F32)

    def absorb(kb, s_ref, diagonal):
        sts = [s_ref[h] for h in range(2)]
        if diagonal:
            key = lax.broadcasted_iota(jnp.int32, (tq, tq), 0)
            qry = lax.broadcasted_iota(jnp.int32, (tq, tq), 1)
            sts = [jnp.where(key <= qry, st, -jnp.inf) for st in sts]
        m_old = [m_ref[h] for h in range(2)]
        m_new = [jnp.maximum(m_old[h], jnp.max(sts[h], axis=0, keepdims=True)) for h in range(2)]
        pts = [jnp.exp(sts[h] - m_new[h]).astype(BF16) for h in range(2)]
        pvs = [jnp.dot(vt_ref[h, kb], pts[h], preferred_element_type=F32) for h in range(2)]
        for h in range(2):
            acc_ref[h] = jnp.exp(m_old[h] - m_new[h]) * acc_ref[h] + pvs[h]
            m_ref[h] = m_new[h]

    scores(0, sa_ref)

    def pair(p, carry):
        kb = 2 * p
        scores(kb + 1, sb_ref)
        absorb(kb, sa_ref, False)
        scores(kb + 2, sa_ref)
        absorb(kb + 1, sb_ref, False)
        return carry

    lax.fori_loop(0, i // 2, pair, 0)

    @pl.when(i % 2 == 0)
    def _():
        absorb(i, sa_ref, True)

    @pl.when(i % 2 == 1)
    def _():
        scores(i, sb_ref)
        absorb(i - 1, sa_ref, False)
        absorb(i, sb_ref, True)

    outs = []
    for h in range(2):
        a = acc_ref[h]
        outs.append(a[:HEAD_DIM] * (1.0 / a[HEAD_DIM:HEAD_DIM + 1]))
    o_ref[...] = jnp.concatenate(outs, axis=0).T.astype(o_ref.dtype)


def _attn_prompt(z, c, tq):
    b, l, _ = z.shape
    hp = ATT_W // LANES
    return pl.pallas_call(
        functools.partial(_attn_kernel, tq=tq),
        grid=(b, hp, l // tq),
        in_specs=[
            pl.BlockSpec((None, tq, LANES), lambda bi, h, i: (bi, i, h)),
            pl.BlockSpec((None, l, LANES), lambda bi, h, i: (bi, 0, hp + h)),
            pl.BlockSpec((None, l, LANES), lambda bi, h, i: (bi, 0, 2 * hp + h)),
            pl.BlockSpec((None, l, LANES), lambda bi, h, i: (bi, 0, 0)),
        ],
        out_specs=[
            pl.BlockSpec((None, tq, LANES), lambda bi, h, i: (bi, i, h)),
            pl.BlockSpec((None, LANES, tq), lambda bi, h, i: (bi, h, i)),
            pl.BlockSpec((None, LANES, tq), lambda bi, h, i: (bi, h, i)),
        ],
        out_shape=[
            jax.ShapeDtypeStruct((b, l, ATT_W), BF16),
            jax.ShapeDtypeStruct((b, ATT_W, l), F32),
            jax.ShapeDtypeStruct((b, ATT_W, l), F32),
        ],
        scratch_shapes=[
            pltpu.VMEM((2, l, LANES), BF16),
            pltpu.VMEM((2, l // tq, V_ROWS, tq), BF16),
            pltpu.VMEM((2, 1, tq), F32),
            pltpu.VMEM((2, V_ROWS, tq), F32),
            pltpu.VMEM((2, tq, tq), F32),
            pltpu.VMEM((2, tq, tq), F32),
        ],
        compiler_params=_params("arbitrary", "arbitrary", "arbitrary"),
        name="attn_prompt",
    )(z, z, z, c)


def _attn_cache_kernel(q_ref, kn_ref, vn_ref, kc_ref, vc_ref, nckc_ref, nckn_ref, o_ref):
    q = q_ref[...]
    lq = q.shape[0]
    masks = _head_masks()
    kc = kc_ref[...].astype(BF16)
    vc = vc_ref[...].astype(BF16)
    kn = kn_ref[...].astype(BF16)
    vn = vn_ref[...].astype(BF16)
    nt_dims = (((1,), (1,)), ((), ()))
    r = lax.broadcasted_iota(jnp.int32, (lq, lq), 0)
    c = lax.broadcasted_iota(jnp.int32, (lq, lq), 1)
    outs = []
    for h in range(2):
        qh = jnp.where(masks[h], q, 0.0).astype(BF16)
        sc = jnp.dot(qh, kc, preferred_element_type=F32) + nckc_ref[pl.ds(h, 1), :]
        sn = lax.dot_general(qh, kn, nt_dims, preferred_element_type=F32) + nckn_ref[pl.ds(h, 1), :]
        sn = jnp.where(c <= r, sn, -jnp.inf)
        m = jnp.maximum(jnp.max(sc, axis=-1, keepdims=True), jnp.max(sn, axis=-1, keepdims=True))
        pc = jnp.exp(sc - m)
        pn = jnp.exp(sn - m)
        den = jnp.sum(pc, axis=-1, keepdims=True) + jnp.sum(pn, axis=-1, keepdims=True)
        acc = (lax.dot_general(pc.astype(BF16), vc, nt_dims, preferred_element_type=F32)
               + jnp.dot(pn.astype(BF16), vn, preferred_element_type=F32))
        outs.append(acc / den)
    o_ref[...] = jnp.where(masks[0], outs[0], outs[1]).astype(o_ref.dtype)


def _attn_sample(z, cache_k, cache_v, layer, nck_c, nck_n):
    b, lq, _ = z.shape
    p = cache_k.shape[3]
    hp = ATT_W // LANES
    return pl.pallas_call(
        _attn_cache_kernel,
        grid=(b, hp),
        in_specs=[
            pl.BlockSpec((None, lq, LANES), lambda bi, h: (bi, 0, h)),
            pl.BlockSpec((None, lq, LANES), lambda bi, h: (bi, 0, hp + h)),
            pl.BlockSpec((None, lq, LANES), lambda bi, h: (bi, 0, 2 * hp + h)),
            pl.BlockSpec((None, None, LANES, p), lambda bi, h: (layer, bi, h, 0)),
            pl.BlockSpec((None, None, LANES, p), lambda bi, h: (layer, bi, h, 0)),
            pl.BlockSpec((None, None, 2, p), lambda bi, h: (bi, h, 0, 0)),
            pl.BlockSpec((None, None, 2, lq), lambda bi, h: (bi, h, 0, 0)),
        ],
        out_specs=pl.BlockSpec((None, lq, LANES), lambda bi, h: (bi, 0, h)),
        out_shape=jax.ShapeDtypeStruct((b, lq, ATT_W), BF16),
        compiler_params=_params("arbitrary", "arbitrary"),
        name="attn_sample",
    )(z, z, z, cache_k, cache_v, nck_c, nck_n)


S5_TILE_GROUPS = LANES // S5_GROUP
S5_TILE_STATE = 2 * S5_TILE_GROUPS * S5_STATE


def _s5_kernel(u_ref, bw_ref, bd_ref, bv_ref, are_ref, aim_ref, d_ref, h0_ref,
               y_ref, send_ref, s_ref, e_ref, sp_ref, *, nb, cb):
    t = pl.program_id(1)
    tc = S5_CHUNK
    rows = nb * cb

    @pl.when(t == 0)
    def _():
        s_ref[...] = h0_ref[...]

    def tokens(j):
        if cb == 1:
            return u_ref[:, j, :]
        return u_ref[:, pl.ds(j, cb, stride=tc), :].reshape(rows, LANES)

    us = [tokens(j) for j in range(tc)]
    ucat = jnp.concatenate([u.astype(BF16) for u in us], axis=1)
    e = jnp.dot(ucat, bw_ref[...], preferred_element_type=F32)
    e_ref[...] = e.reshape(nb, cb, S5_TILE_STATE)

    a_re = are_ref[...]
    a_im = aim_ref[...]

    def advance(c, carry):
        for b in range(nb):
            s = s_ref[b]
            sp_ref[b, pl.ds(c, 1), :] = s
            s_ref[b] = a_re * s + a_im * pltpu.roll(s, S5_TILE_STATE // 2, 1) + e_ref[b, pl.ds(c, 1), :]
        return carry

    lax.fori_loop(0, cb, advance, 0)

    sp = sp_ref[...].reshape(rows, S5_TILE_STATE).astype(BF16)
    d = d_ref[...]
    y_state = jnp.dot(sp, bv_ref[...], preferred_element_type=F32)
    pair = 2 * LANES
    for k in range(tc // 2):
        kk = pair * (k + 1)
        yk = jnp.dot(ucat[:, :kk], bd_ref[:kk, k * pair:(k + 1) * pair], preferred_element_type=F32)
        yk = yk + y_state[:, k * pair:(k + 1) * pair]
        for jj in range(2):
            jo = 2 * k + jj
            y = yk[:, jj * LANES:(jj + 1) * LANES] + us[jo] * d
            if cb == 1:
                y_ref[:, jo, :] = y
            else:
                y_ref[:, pl.ds(jo, cb, stride=tc), :] = y.reshape(nb, cb, LANES)

    @pl.when(t == pl.num_programs(1) - 1)
    def _():
        send_ref[...] = s_ref[...]


def _s5(z3, u_col, mats, h0, cb):
    bw, bd, bv, a_re, a_im, d = mats
    nb, l, _ = z3.shape
    tiles = bw.shape[0]
    tc = S5_CHUNK
    blk = tc * cb
    col0 = u_col // LANES
    return pl.pallas_call(
        functools.partial(_s5_kernel, nb=nb, cb=cb),
        grid=(tiles, l // blk),
        in_specs=[
            pl.BlockSpec((nb, blk, LANES), lambda g, t: (0, t, col0 + g)),
            pl.BlockSpec((None, tc * LANES, S5_TILE_STATE), lambda g, t: (g, 0, 0)),
            pl.BlockSpec((None, tc * LANES, tc * LANES), lambda g, t: (g, 0, 0)),
            pl.BlockSpec((None, S5_TILE_STATE, tc * LANES), lambda g, t: (g, 0, 0)),
            pl.BlockSpec((None, 1, S5_TILE_STATE), lambda g, t: (g, 0, 0)),
            pl.BlockSpec((None, 1, S5_TILE_STATE), lambda g, t: (g, 0, 0)),
            pl.BlockSpec((None, 1, LANES), lambda g, t: (g, 0, 0)),
            pl.BlockSpec((nb, None, 1, S5_TILE_STATE), lambda g, t: (0, g, 0, 0)),
        ],
        out_specs=[
            pl.BlockSpec((nb, blk, LANES), lambda g, t: (0, t, g)),
            pl.BlockSpec((nb, None, 1, S5_TILE_STATE), lambda g, t: (0, g, 0, 0)),
        ],
        out_shape=[jax.ShapeDtypeStruct((nb, l, tiles * LANES), F32), jax.ShapeDtypeStruct(h0.shape, F32)],
        scratch_shapes=[
            pltpu.VMEM((nb, 1, S5_TILE_STATE), F32),
            pltpu.VMEM((nb, cb, S5_TILE_STATE), F32),
            pltpu.VMEM((nb, cb, S5_TILE_STATE), F32),
        ],
        compiler_params=_params("arbitrary", "arbitrary"),
        name="s5",
    )(z3, bw, bd, bv, a_re, a_im, d, h0)


def _s5_matrices(lam_re, lam_im, log_dt, b_re, b_im, c_re, c_im, d_skip):
    g, p = lam_re.shape
    tc = S5_CHUNK
    dt = jnp.exp(log_dt)[:, None]
    k = jnp.arange(tc + 1, dtype=F32)[:, None, None]
    mag = jnp.exp(lam_re * dt * k)
    pw_re = mag * jnp.cos(lam_im * dt * k)
    pw_im = mag * jnp.sin(lam_im * dt * k)
    den = lam_re * lam_re + lam_im * lam_im
    nr = pw_re[1] - 1.0
    f_re = (nr * lam_re + pw_im[1] * lam_im) / den
    f_im = (pw_im[1] * lam_re - nr * lam_im) / den
    bb_re = f_re[..., None] * b_re - f_im[..., None] * b_im
    bb_im = f_re[..., None] * b_im + f_im[..., None] * b_re
    ein = functools.partial(jnp.einsum, precision=HIGHEST)
    cp_re = c_re[None] * pw_re[:tc, :, None, :] - c_im[None] * pw_im[:tc, :, None, :]
    cp_im = c_re[None] * pw_im[:tc, :, None, :] + c_im[None] * pw_re[:tc, :, None, :]
    taps = ein('tghp,gpk->tghk', cp_re, bb_re) - ein('tghp,gpk->tghk', cp_im, bb_im)
    k_rev = (tc - 1) - jnp.arange(tc, dtype=F32)[:, None, None]
    mag_rev = jnp.exp(lam_re * dt * k_rev)
    rp_re = (mag_rev * jnp.cos(lam_im * dt * k_rev))[:, :, :, None]
    rp_im = (mag_rev * jnp.sin(lam_im * dt * k_rev))[:, :, :, None]
    w_re = rp_re * bb_re[None] - rp_im * bb_im[None]
    w_im = rp_re * bb_im[None] + rp_im * bb_re[None]
    z_re = c_re[None] * pw_re[1:, :, None, :] - c_im[None] * pw_im[1:, :, None, :]
    z_im = c_re[None] * pw_im[1:, :, None, :] + c_im[None] * pw_re[1:, :, None, :]
    gl = S5_TILE_GROUPS
    tiles = g // gl
    h = S5_GROUP
    tok_lane = jnp.arange(tc * LANES)
    st_lane = jnp.arange(S5_TILE_STATE)
    tok_group = (tok_lane // h) % gl
    st_group = (st_lane // p) % gl
    cmp_lane = jnp.arange(2 * p)
    rep_state = ((st_lane[None, :] // (gl * p) == cmp_lane[:, None] // p)
                 & (st_lane[None, :] % p == cmp_lane[:, None] % p)).astype(BF16)

    def expand(values, rep, row_group, col_group, rep_rows=False):
        spec = 'rq,tqc->trc' if rep_rows else 'trq,qc->trc'
        args = (rep.T, values.astype(BF16)) if rep_rows else (values.astype(BF16), rep)
        wide = jnp.einsum(spec, *args, preferred_element_type=F32)
        return jnp.where(row_group[:, None] == col_group[None, :], wide, 0.0).astype(BF16)

    tp = taps.reshape(tc, tiles, gl, h, h).transpose(1, 0, 2, 4, 3)
    eye = jnp.eye(gl, dtype=F32)
    bd = (tp[:, :, :, :, None, :] * eye[None, None, :, None, :, None]).reshape(tiles, tc, LANES, LANES).astype(BF16)
    lag = jnp.arange(tc)[None, :] - jnp.arange(tc)[:, None]
    bd = jnp.where((lag >= 0)[None, :, :, None, None], bd[:, jnp.clip(lag, 0, tc - 1)], 0)
    bd = bd.transpose(0, 1, 3, 2, 4).reshape(tiles, tc * LANES, tc * LANES)
    ws = jnp.stack([w_re, w_im]).reshape(2, tc, tiles, gl, p, h).transpose(2, 1, 3, 5, 0, 4)
    bw = expand(ws.reshape(tiles, tc * LANES, 2 * p), rep_state, tok_group, st_group)
    zs = jnp.stack([z_re, -z_im]).reshape(2, tc, tiles, gl, h, p).transpose(2, 0, 5, 1, 3, 4)
    bv = expand(zs.reshape(tiles, 2 * p, tc * LANES), rep_state, st_group, tok_group, rep_rows=True)
    ar = pw_re[tc].reshape(tiles, gl * p)
    ai = pw_im[tc].reshape(tiles, gl * p)
    a_re = jnp.concatenate([ar, ar], axis=-1)[:, None, :]
    a_im = jnp.concatenate([-ai, ai], axis=-1)[:, None, :]
    d = d_skip.reshape(tiles, 1, LANES)
    return bw.astype(BF16), bd.astype(BF16), bv.astype(BF16), a_re, a_im, d


def _s5_pack_state(re, im):
    b, g, p = re.shape
    tiles = g // S5_TILE_GROUPS
    return jnp.concatenate([re.reshape(b, tiles, 1, -1), im.reshape(b, tiles, 1, -1)], axis=-1)


def _s5_unpack_state(s, g):
    b = s.shape[0]
    half = S5_TILE_STATE // 2
    return s[..., :half].reshape(b, g, S5_STATE), s[..., half:].reshape(b, g, S5_STATE)


def _mix_kernel(o_ref, y_ref, ga_ref, gb_ref, wa_ref, wv_ref, wg_ref, out_ref, g5_ref):
    @pl.when(pl.program_id(1) == 0)
    def _():
        g5_ref[...] = _gelu_tanh(y_ref[...].astype(F32)).astype(BF16)

    ya = jnp.dot(o_ref[...], wa_ref[...], preferred_element_type=F32)
    g5 = g5_ref[...]
    yb = (jnp.dot(g5, wv_ref[...], preferred_element_type=F32)
          * _sigmoid(jnp.dot(g5, wg_ref[...], preferred_element_type=F32)))
    out = _sigmoid(ga_ref[...]) * ya + _sigmoid(gb_ref[...]) * yb
    out_ref[...] = out.astype(out_ref.dtype)


def _mix(o, y, z, gate_col, wa, wv, wg, tm, tn):
    t, kdim = o.shape
    n = wa.shape[1]
    ga0 = gate_col // tn
    gb0 = (gate_col + n) // tn
    return pl.pallas_call(
        _mix_kernel,
        grid=(t // tm, n // tn),
        in_specs=[
            pl.BlockSpec((tm, kdim), lambda i, j: (i, 0)),
            pl.BlockSpec((tm, kdim), lambda i, j: (i, 0)),
            pl.BlockSpec((tm, tn), lambda i, j: (i, ga0 + j)),
            pl.BlockSpec((tm, tn), lambda i, j: (i, gb0 + j)),
            pl.BlockSpec((kdim, tn), lambda i, j: (0, j)),
            pl.BlockSpec((kdim, tn), lambda i, j: (0, j)),
            pl.BlockSpec((kdim, tn), lambda i, j: (0, j)),
        ],
        out_specs=pl.BlockSpec((tm, tn), lambda i, j: (i, j)),
        out_shape=jax.ShapeDtypeStruct((t, n), BF16),
        scratch_shapes=[pltpu.VMEM((tm, kdim), BF16)],
        compiler_params=_params("arbitrary", "arbitrary"),
        name="mix",
    )(o, y, z, z, wa, wv, wg)


def _mm_norm_kernel(a_ref, w_ref, x_ref, g_ref, out_ref, *, k_steps):
    def product():
        return jnp.dot(a_ref[...], w_ref[...], preferred_element_type=F32)

    def finish(m):
        out_ref[...] = x_ref[...] + (m * _rms_scale(m)) * g_ref[...]

    if k_steps == 1:
        finish(product())
        return
    k = pl.program_id(1)

    @pl.when(k == 0)
    def _():
        out_ref[...] = product()

    @pl.when((k > 0) & (k < k_steps - 1))
    def _():
        out_ref[...] += product()

    @pl.when(k == k_steps - 1)
    def _():
        finish(out_ref[...] + product())


def _mm_norm(a, w, x, g, tm, tk):
    t, kdim = a.shape
    n = w.shape[1]
    return pl.pallas_call(
        functools.partial(_mm_norm_kernel, k_steps=kdim // tk),
        grid=(t // tm, kdim // tk),
        in_specs=[
            pl.BlockSpec((tm, tk), lambda i, k: (i, k)),
            pl.BlockSpec((tk, n), lambda i, k: (k, 0)),
            pl.BlockSpec((tm, n), lambda i, k: (i, 0)),
            pl.BlockSpec((1, n), lambda i, k: (0, 0)),
        ],
        out_specs=pl.BlockSpec((tm, n), lambda i, k: (i, 0)),
        out_shape=jax.ShapeDtypeStruct((t, n), F32),
        compiler_params=_params("arbitrary", "arbitrary"),
        name="mm_norm",
    )(a, w, x, g)


def _ffn_up_kernel(x_ref, g_ref, wa_ref, wb_ref, cw_ref, buf_ref, act_ref, nbuf_ref, h_ref, tail_ref,
                   *, rows_outer, blocks_per_seq):
    if rows_outer:
        i, j = pl.program_id(0), pl.program_id(1)
        fresh_rows = j == 0
    else:
        j, i = pl.program_id(0), pl.program_id(1)
        fresh_rows = True

    def normalise():
        x = x_ref[...]
        h_ref[...] = ((x * _rms_scale(x)) * g_ref[...]).astype(BF16)

    if rows_outer:
        pl.when(fresh_rows)(normalise)
    else:
        normalise()

    if blocks_per_seq > 1:
        @pl.when((i == 0) & (j == 0))
        def _():
            tail_ref[...] = jnp.zeros_like(tail_ref)

    h = h_ref[...]
    a = jnp.dot(h, wa_ref[...], preferred_element_type=F32)
    b = jnp.dot(h, wb_ref[...], preferred_element_type=F32)
    tm = a.shape[0]
    if blocks_per_seq == 1:
        prev = buf_ref[...]
    else:
        prev = jnp.where(i % blocks_per_seq == 0, buf_ref[...], tail_ref[j])
    last = a[tm - SUBLANES:, :]
    tail_ref[j] = last
    nbuf_ref[...] = last
    row = lax.broadcasted_iota(jnp.int32, a.shape, 0)
    p1 = prev[SUBLANES - 1:SUBLANES, :]
    p2 = prev[SUBLANES - 2:SUBLANES - 1, :]
    a1 = jnp.where(row == 0, p1, pltpu.roll(a, 1, 0))
    a2 = jnp.where(row == 0, p2, jnp.where(row == 1, p1, pltpu.roll(a, 2, 0)))
    cw = cw_ref[...]
    conv = a2 * cw[0:1, :] + a1 * cw[1:2, :] + a * cw[2:3, :]
    act_ref[...] = (_gelu_tanh(conv) * b).astype(act_ref.dtype)


def _ffn_up(x, g, w_up, cw, buf, seq_len, tm, tn, rows_outer):
    t, d = x.shape
    f = w_up.shape[1] // 2
    nj = f // tn
    bps = seq_len // tm
    if rows_outer:
        grid = (t // tm, nj)
        ij = lambda a, b: (a, b)
    else:
        grid = (nj, t // tm)
        ij = lambda a, b: (b, a)

    def spec(shape, fn):
        return pl.BlockSpec(shape, lambda a, b: fn(*ij(a, b)))

    return pl.pallas_call(
        functools.partial(_ffn_up_kernel, rows_outer=rows_outer, blocks_per_seq=bps),
        grid=grid,
        in_specs=[
            spec((tm, d), lambda i, j: (i, 0)),
            spec((1, d), lambda i, j: (0, 0)),
            spec((d, tn), lambda i, j: (0, j)),
            spec((d, tn), lambda i, j: (0, nj + j)),
            spec((SUBLANES, tn), lambda i, j: (0, j)),
            spec((None, SUBLANES, tn), lambda i, j: (i // bps, 0, j)),
        ],
        out_specs=[
            spec((tm, tn), lambda i, j: (i, j)),
            spec((None, SUBLANES, tn), lambda i, j: (i, 0, j)),
        ],
        out_shape=[jax.ShapeDtypeStruct((t, f), BF16), jax.ShapeDtypeStruct((t // tm, SUBLANES, f), F32)],
        scratch_shapes=[pltpu.VMEM((tm, d), BF16), pltpu.VMEM((nj, SUBLANES, tn), F32)],
        compiler_params=_params("arbitrary", "arbitrary"),
        name="ffn_up",
    )(x, g, w_up, w_up, cw, buf)


def _pick(t, pref):
    return pref if t % pref == 0 else t


def _tile_plan(t, l, d, f):
    tm = _pick(t, 1024)
    long_seq = l % tm == 0
    return dict(
        tm=tm,
        tn=512,
        tq=512,
        long_seq=long_seq,
        tm_up=tm if long_seq else l,
        s5_chunks=min(64, l // S5_CHUNK),
        tm_mm=_pick(t, 512),
        tk_out=d,
        tk_down=f // 4 if (f // 4) % LANES == 0 else 512,
    )


def _trunk(x, weights, caches):
    b, l, d = x.shape
    t = b * l
    depth = weights['w_main'].shape[0]
    n_main = weights['w_main'].shape[2]
    gate_col = n_main - 2 * d
    u_col = 3 * ATT_W
    g_s5 = weights['lam_re'].shape[1]
    f = weights['w_down'].shape[1]
    tp = _tile_plan(t, l, d, f)
    tm, tn, tm_up, long_seq = tp['tm'], tp['tn'], tp['tm_up'], tp['long_seq']
    hp = ATT_W // LANES
    x = x.reshape(t, d)
    st = {k: [] for k in ('k', 'v', 'logf', 're', 'im', 'conv')}
    for layer in range(depth):
        wl = {k: v[layer] for k, v in weights.items()}
        z, logf = _norm_in(x, wl['g_mix_pre'], wl['w_main'], wl['w_f'], wl['b_f'], tm, tn)
        st['logf'].append(logf[:, :N_HEADS].reshape(b, l, N_HEADS))
        z3 = z.reshape(b, l, n_main)
        if caches is None:
            c = _cumsum_rows(logf.reshape(b, l, LANES), CUMSUM_ROWS)
            o, kt, vt = _attn_prompt(z3, c, tp['tq'])
            st['k'].append(kt.reshape(b, N_HEADS, HEAD_DIM, l))
            st['v'].append(vt.reshape(b, N_HEADS, HEAD_DIM, l))
            h0 = jnp.zeros((b, g_s5 // S5_TILE_GROUPS, 1, S5_TILE_STATE), F32)
            buf = jnp.zeros((b, SUBLANES, f), F32)
        else:
            st['k'].append(z[:, ATT_W:2 * ATT_W].reshape(b, l, N_HEADS, HEAD_DIM))
            st['v'].append(z[:, 2 * ATT_W:3 * ATT_W].reshape(b, l, N_HEADS, HEAD_DIM))
            past = caches['logf'].shape[2]
            lf_c = jnp.pad(caches['logf'][layer], ((0, 0), (0, 0), (0, LANES - N_HEADS)))
            lf_all = jnp.concatenate([lf_c, logf.reshape(b, l, LANES)], axis=1)
            padded = -(-(past + l) // CUMSUM_ROWS) * CUMSUM_ROWS
            lf_all = jnp.pad(lf_all, ((0, 0), (0, padded - past - l), (0, 0)))
            nck_all = -_cumsum_rows(lf_all, CUMSUM_ROWS)[:, :, :N_HEADS].transpose(0, 2, 1)
            nck_c = nck_all[:, :, :past].reshape(b, hp, 2, past)
            nck_n = nck_all[:, :, past:past + l].reshape(b, hp, 2, l)
            o = _attn_sample(z3, caches['k'], caches['v'], layer, nck_c, nck_n)
            h0 = _s5_pack_state(caches['ssm_re'][layer], caches['ssm_im'][layer])
            buf = jnp.pad(caches['conv'][layer], ((0, 0), (SUBLANES - (CONV_W - 1), 0), (0, 0)))
        o = o.reshape(t, ATT_W)
        y, s_end = _s5(z3, u_col, weights['s5_mats'][layer], h0, tp['s5_chunks'])
        y = y.reshape(t, g_s5 * S5_GROUP)
        s_re, s_im = _s5_unpack_state(s_end, g_s5)
        st['re'].append(s_re)
        st['im'].append(s_im)
        mix_in = _mix(o, y, z, gate_col, wl['w_att_proj'], wl['w_glu_v'], wl['w_glu_g'], tm, tn)
        x = _mm_norm(mix_in, wl['w_out'], x, wl['g_mix_post'], tp['tm_mm'], tp['tk_out'])
        act, nbuf = _ffn_up(x, wl['g_ffn_pre'], wl['w_up'], wl['conv_w'], buf, l, tm_up, tn, long_seq)
        nbuf = nbuf.reshape(b, l // tm_up, SUBLANES, f)[:, -1]
        st['conv'].append(nbuf[:, SUBLANES - (CONV_W - 1):])
        x = _mm_norm(act, wl['w_down'], x, wl['g_ffn_post'], tp['tm_mm'], tp['tk_down'])
    stacked = [jnp.stack(st[k]) for k in ('k', 'v', 'logf', 're', 'im', 'conv')]
    if caches is None:
        stacked[0] = stacked[0].transpose(0, 1, 4, 2, 3)
        stacked[1] = stacked[1].transpose(0, 1, 4, 2, 3)
    return x.reshape(b, l, d), stacked


def kernel(x_prompt, x_sample, cache_k, cache_v, cache_logf, state_ssm_re, state_ssm_im, state_conv,
           g_mix_pre, w_in, b_f, lam_re, lam_im, log_dt, b_re, b_im, c_re, c_im, d_skip,
           w_att_proj, w_glu_v, w_glu_g, w_out, g_mix_post, g_ffn_pre, w_up, conv_w, w_down, g_ffn_post):
    depth, d, _ = w_in.shape
    s5_w = d_skip.shape[1]
    f_lo, f_hi = 3 * ATT_W, 3 * ATT_W + N_HEADS
    scale = HEAD_DIM ** -0.5
    w_main = jnp.concatenate([w_in[:, :, :ATT_W] * scale, w_in[:, :, ATT_W:f_lo], w_in[:, :, f_hi:]], axis=2)
    weights = {
        'w_main': w_main.astype(BF16),
        'w_f': jnp.pad(w_in[:, :, f_lo:f_hi], ((0, 0), (0, 0), (0, LANES - N_HEADS))).astype(BF16),
        'b_f': jnp.pad(b_f, ((0, 0), (0, LANES - N_HEADS))).reshape(depth, 1, LANES),
        'g_mix_pre': g_mix_pre.reshape(depth, 1, d),
        'g_mix_post': g_mix_post.reshape(depth, 1, d),
        'g_ffn_pre': g_ffn_pre.reshape(depth, 1, d),
        'g_ffn_post': g_ffn_post.reshape(depth, 1, d),
        'lam_re': lam_re,
        's5_mats': [_s5_matrices(lam_re[n], lam_im[n], log_dt[n], b_re[n], b_im[n], c_re[n], c_im[n], d_skip[n])
                    for n in range(depth)],
        'w_att_proj': w_att_proj.astype(BF16),
        'w_glu_v': w_glu_v.astype(BF16),
        'w_glu_g': w_glu_g.astype(BF16),
        'w_out': w_out.astype(BF16),
        'w_up': w_up.astype(BF16),
        'conv_w': jnp.pad(conv_w, ((0, 0), (0, SUBLANES - CONV_W), (0, 0))),
        'w_down': w_down.astype(BF16),
    }
    assert s5_w == lam_re.shape[1] * S5_GROUP
    sb, sp = cache_k.shape[1], cache_k.shape[2]
    caches = {
        'k': cache_k.transpose(0, 1, 3, 4, 2).reshape(depth, sb, ATT_W, sp),
        'v': cache_v.transpose(0, 1, 3, 4, 2).reshape(depth, sb, ATT_W, sp),
        'logf': cache_logf, 'ssm_re': state_ssm_re, 'ssm_im': state_ssm_im, 'conv': state_conv,
    }
    y_prompt, p_st = _trunk(x_prompt, weights, None)
    y_sample, s_st = _trunk(x_sample, weights, caches)
    return (y_prompt, y_sample, *p_st, *s_st)
```

```python
import functools

import jax
import jax.numpy as jnp
from jax import lax
from jax.experimental import pallas as pl
from jax.experimental.pallas import tpu as pltpu

F32 = jnp.float32
BF16 = jnp.bfloat16
HIGHEST = lax.Precision.HIGHEST

RMS_EPS = 1e-6
N_HEADS = 16
HEAD_DIM = 64
ATT_W = N_HEADS * HEAD_DIM
S5_GROUP = 16
S5_STATE = 64
CONV_W = 3
LANES = 128
SUBLANES = 8
S5_CHUNK = 16
VMEM_LIMIT_BYTES = 56 * 1024 * 1024


def _params(*sem):
    return pltpu.CompilerParams(dimension_semantics=sem, vmem_limit_bytes=VMEM_LIMIT_BYTES)


def _gelu_tanh(x):
    return 0.5 * x * (1.0 + jnp.tanh(0.7978845608028654 * (x + 0.044715 * (x * x * x))))


def _sigmoid(x):
    return 0.5 * jnp.tanh(0.5 * x) + 0.5


def _log_sigmoid(x):
    return jnp.minimum(x, 0.0) - jnp.log1p(jnp.exp(-jnp.abs(x)))


def _rms_scale(x):
    return lax.rsqrt(jnp.mean(x * x, axis=-1, keepdims=True) + RMS_EPS)


def _norm_in_kernel(x_ref, g_ref, w_ref, wf_ref, bf_ref, z_ref, lf_ref, h_ref):
    @pl.when(pl.program_id(1) == 0)
    def _():
        x = x_ref[...]
        h = ((x * _rms_scale(x)) * g_ref[...]).astype(BF16)
        h_ref[...] = h
        fl = jnp.dot(h, wf_ref[...], preferred_element_type=F32) + bf_ref[...]
        lf_ref[...] = _log_sigmoid(fl)

    z_ref[...] = jnp.dot(h_ref[...], w_ref[...], preferred_element_type=F32)


def _norm_in(x, g, w, wf, bf, tm, tn):
    t, d = x.shape
    n = w.shape[1]
    return pl.pallas_call(
        _norm_in_kernel,
        grid=(t // tm, n // tn),
        in_specs=[
            pl.BlockSpec((tm, d), lambda i, j: (i, 0)),
            pl.BlockSpec((1, d), lambda i, j: (0, 0)),
            pl.BlockSpec((d, tn), lambda i, j: (0, j)),
            pl.BlockSpec((d, LANES), lambda i, j: (0, 0)),
            pl.BlockSpec((1, LANES), lambda i, j: (0, 0)),
        ],
        out_specs=[
            pl.BlockSpec((tm, tn), lambda i, j: (i, j)),
            pl.BlockSpec((tm, LANES), lambda i, j: (i, 0)),
        ],
        out_shape=[jax.ShapeDtypeStruct((t, n), F32), jax.ShapeDtypeStruct((t, LANES), F32)],
        scratch_shapes=[pltpu.VMEM((tm, d), BF16)],
        compiler_params=_params("arbitrary", "arbitrary"),
        name="norm_in",
    )(x, g, w, wf, bf)


CUMSUM_ROWS = 512


def _cumsum_rows_kernel(x_ref, o_ref, carry_ref):
    @pl.when(pl.program_id(1) == 0)
    def _():
        carry_ref[...] = jnp.zeros_like(carry_ref)

    n = x_ref.shape[0]
    row = lax.broadcasted_iota(jnp.int32, (n, n), 0)
    col = lax.broadcasted_iota(jnp.int32, (n, n), 1)
    lower = (col <= row).astype(F32)
    c = jnp.dot(lower, x_ref[...], precision=HIGHEST, preferred_element_type=F32) + carry_ref[...]
    o_ref[...] = c
    carry_ref[...] = c[n - 1:n, :]


def _cumsum_rows(x, blk):
    b, l, w = x.shape
    return pl.pallas_call(
        _cumsum_rows_kernel,
        grid=(b, l // blk),
        in_specs=[pl.BlockSpec((None, blk, w), lambda i, j: (i, j, 0))],
        out_specs=pl.BlockSpec((None, blk, w), lambda i, j: (i, j, 0)),
        out_shape=jax.ShapeDtypeStruct((b, l, w), F32),
        scratch_shapes=[pltpu.VMEM((1, w), F32)],
        compiler_params=_params("arbitrary", "arbitrary"),
        name="cumsum_rows",
    )(x)


def _head_masks():
    lane = lax.broadcasted_iota(jnp.int32, (1, LANES), 1)
    return lane < HEAD_DIM, lane >= HEAD_DIM


BIAS_PARTS = 3
V_ROWS = HEAD_DIM + 16


def _attn_kernel(q_ref, k_ref, v_ref, c_ref, o_ref, kt_ref, vto_ref,
                 kx_ref, vt_ref, m_ref, acc_ref, sa_ref, sb_ref, *, tq, tk):
    hp = pl.program_id(1)
    i = pl.program_id(2)
    n_blk = k_ref.shape[0] // tk
    lane = lax.broadcasted_iota(jnp.int32, (1, LANES), 1)
    own = [(lane >= h * HEAD_DIM) & (lane < (h + 1) * HEAD_DIM) for h in range(2)]
    spare = [(1 - h) * HEAD_DIM for h in range(2)]

    @pl.when(i == 0)
    def _():
        er = lax.broadcasted_iota(jnp.int32, (LANES, LANES), 0)
        ec = lax.broadcasted_iota(jnp.int32, (LANES, LANES), 1)
        row = lax.broadcasted_iota(jnp.int32, (LANES, tk), 0)

        def prep(j, carry):
            r0 = pl.multiple_of(j * tk, tk)
            kblk = k_ref[pl.ds(r0, tk), :]
            nc = -c_ref[pl.ds(r0, tk), :]
            parts = []
            rem = nc
            for _ in range(BIAS_PARTS):
                piece = rem.astype(BF16)
                parts.append(piece)
                rem = rem - piece.astype(F32)
            vt = v_ref[pl.ds(r0, tk), :].T
            for h in range(2):
                bias = jnp.zeros((tk, LANES), F32)
                for n, piece in enumerate(parts):
                    sel = ((er == 2 * hp + h) & (ec == spare[h] + n)).astype(BF16)
                    bias = bias + jnp.dot(piece, sel, preferred_element_type=F32)
                kx_ref[h, pl.ds(r0, tk), :] = (jnp.where(own[h], kblk, 0.0) + bias).astype(BF16)
                vth = vt if h == 0 else pltpu.roll(vt, HEAD_DIM, 0)
                vt_ref[h, j] = jnp.where(row < HEAD_DIM, vth, 1.0)[:V_ROWS].astype(BF16)
            return carry

        lax.fori_loop(0, n_blk, prep, 0)

    own_rows = pl.ds(pl.multiple_of(i * tq, tq), tq)
    kt_ref[...] = k_ref[own_rows, :].T
    vto_ref[...] = v_ref[own_rows, :].T

    q = q_ref[...]
    qx = []
    for h in range(2):
        ones = (lane >= spare[h]) & (lane < spare[h] + BIAS_PARTS)
        qx.append((jnp.where(own[h], q, 0.0) + jnp.where(ones, 1.0, 0.0)).astype(BF16))
    m_ref[...] = jnp.full_like(m_ref, -jnp.inf)
    acc_ref[...] = jnp.zeros_like(acc_ref)
    nt_dims = (((1,), (1,)), ((), ()))

    def scores(kb, s_ref):
        start = pl.multiple_of(kb * tk, tk)
        for h in range(2):
            s_ref[h] = lax.dot_general(kx_ref[h, pl.ds(start, tk), :], qx[h], nt_dims,
                                       preferred_element_type=F32)

    def absorb(kb, s_ref, first_key=None):
        sts = [s_ref[h] for h in range(2)]
        if first_key is not None:
            key = lax.broadcasted_iota(jnp.int32, (tk, tq), 0) + first_key
            qry = lax.broadcasted_iota(jnp.int32, (tk, tq), 1)
            sts = [jnp.where(key <= qry, st, -jnp.inf) for st in sts]
        m_old = [m_ref[h] for h in range(2)]
        m_new = [jnp.maximum(m_old[h], jnp.max(sts[h], axis=0, keepdims=True)) for h in range(2)]
        pts = [jnp.exp(sts[h] - m_new[h]).astype(BF16) for h in range(2)]
        pvs = [jnp.dot(vt_ref[h, kb], pts[h], preferred_element_type=F32) for h in range(2)]
        for h in range(2):
            acc_ref[h] = jnp.exp(m_old[h] - m_new[h]) * acc_ref[h] + pvs[h]
            m_ref[h] = m_new[h]

    scores(0, sa_ref)

    def pair(p, carry):
        kb = 2 * p
        scores(kb + 1, sb_ref)
        absorb(kb, sa_ref)
        scores(kb + 2, sa_ref)
        absorb(kb + 1, sb_ref)
        return carry

    lax.fori_loop(0, i, pair, 0)
    scores(2 * i + 1, sb_ref)
    absorb(2 * i, sa_ref, first_key=0)
    absorb(2 * i + 1, sb_ref, first_key=tk)

    outs = []
    for h in range(2):
        a = acc_ref[h]
        outs.append(a[:HEAD_DIM] * (1.0 / a[HEAD_DIM:HEAD_DIM + 1]))
    o_ref[...] = jnp.concatenate(outs, axis=0).T.astype(o_ref.dtype)


def _attn_prompt(z, c, tq):
    b, l, _ = z.shape
    hp = ATT_W // LANES
    tk = tq // 2
    return pl.pallas_call(
        functools.partial(_attn_kernel, tq=tq, tk=tk),
        grid=(b, hp, l // tq),
        in_specs=[
            pl.BlockSpec((None, tq, LANES), lambda bi, h, i: (bi, i, h)),
            pl.BlockSpec((None, l, LANES), lambda bi, h, i: (bi, 0, hp + h)),
            pl.BlockSpec((None, l, LANES), lambda bi, h, i: (bi, 0, 2 * hp + h)),
            pl.BlockSpec((None, l, LANES), lambda bi, h, i: (bi, 0, 0)),
        ],
        out_specs=[
            pl.BlockSpec((None, tq, LANES), lambda bi, h, i: (bi, i, h)),
            pl.BlockSpec((None, LANES, tq), lambda bi, h, i: (bi, h, i)),
            pl.BlockSpec((None, LANES, tq), lambda bi, h, i: (bi, h, i)),
        ],
        out_shape=[
            jax.ShapeDtypeStruct((b, l, ATT_W), BF16),
            jax.ShapeDtypeStruct((b, ATT_W, l), F32),
            jax.ShapeDtypeStruct((b, ATT_W, l), F32),
        ],
        scratch_shapes=[
            pltpu.VMEM((2, l, LANES), BF16),
            pltpu.VMEM((2, l // tk, V_ROWS, tk), BF16),
            pltpu.VMEM((2, 1, tq), F32),
            pltpu.VMEM((2, V_ROWS, tq), F32),
            pltpu.VMEM((2, tk, tq), F32),
            pltpu.VMEM((2, tk, tq), F32),
        ],
        compiler_params=_params("arbitrary", "arbitrary", "arbitrary"),
        name="attn_prompt",
    )(z, z, z, c)


def _attn_cache_kernel(q_ref, kn_ref, vn_ref, kc_ref, vc_ref, nckc_ref, nckn_ref, o_ref):
    q = q_ref[...]
    lq = q.shape[0]
    masks = _head_masks()
    kc = kc_ref[...].astype(BF16)
    vc = vc_ref[...].astype(BF16)
    kn = kn_ref[...].astype(BF16)
    vn = vn_ref[...].astype(BF16)
    nt_dims = (((1,), (1,)), ((), ()))
    r = lax.broadcasted_iota(jnp.int32, (lq, lq), 0)
    c = lax.broadcasted_iota(jnp.int32, (lq, lq), 1)
    outs = []
    for h in range(2):
        qh = jnp.where(masks[h], q, 0.0).astype(BF16)
        sc = jnp.dot(qh, kc, preferred_element_type=F32) + nckc_ref[pl.ds(h, 1), :]
        sn = lax.dot_general(qh, kn, nt_dims, preferred_element_type=F32) + nckn_ref[pl.ds(h, 1), :]
        sn = jnp.where(c <= r, sn, -jnp.inf)
        m = jnp.maximum(jnp.max(sc, axis=-1, keepdims=True), jnp.max(sn, axis=-1, keepdims=True))
        pc = jnp.exp(sc - m)
        pn = jnp.exp(sn - m)
        den = jnp.sum(pc, axis=-1, keepdims=True) + jnp.sum(pn, axis=-1, keepdims=True)
        acc = (lax.dot_general(pc.astype(BF16), vc, nt_dims, preferred_element_type=F32)
               + jnp.dot(pn.astype(BF16), vn, preferred_element_type=F32))
        outs.append(acc / den)
    o_ref[...] = jnp.where(masks[0], outs[0], outs[1]).astype(o_ref.dtype)


def _attn_sample(z, cache_k, cache_v, layer, nck_c, nck_n):
    b, lq, _ = z.shape
    p = cache_k.shape[3]
    hp = ATT_W // LANES
    return pl.pallas_call(
        _attn_cache_kernel,
        grid=(b, hp),
        in_specs=[
            pl.BlockSpec((None, lq, LANES), lambda bi, h: (bi, 0, h)),
            pl.BlockSpec((None, lq, LANES), lambda bi, h: (bi, 0, hp + h)),
            pl.BlockSpec((None, lq, LANES), lambda bi, h: (bi, 0, 2 * hp + h)),
            pl.BlockSpec((None, None, LANES, p), lambda bi, h: (layer, bi, h, 0)),
            pl.BlockSpec((None, None, LANES, p), lambda bi, h: (layer, bi, h, 0)),
            pl.BlockSpec((None, None, 2, p), lambda bi, h: (bi, h, 0, 0)),
            pl.BlockSpec((None, None, 2, lq), lambda bi, h: (bi, h, 0, 0)),
        ],
        out_specs=pl.BlockSpec((None, lq, LANES), lambda bi, h: (bi, 0, h)),
        out_shape=jax.ShapeDtypeStruct((b, lq, ATT_W), BF16),
        compiler_params=_params("arbitrary", "arbitrary"),
        name="attn_sample",
    )(z, z, z, cache_k, cache_v, nck_c, nck_n)


S5_TILE_GROUPS = LANES // S5_GROUP
S5_TILE_STATE = 2 * S5_TILE_GROUPS * S5_STATE


def _s5_kernel(u_ref, bw_ref, bd_ref, bv_ref, are_ref, aim_ref, d_ref, h0_ref,
               y_ref, send_ref, s_ref, e_ref, sp_ref, *, nb, cb):
    t = pl.program_id(1)
    tc = S5_CHUNK
    rows = nb * cb

    @pl.when(t == 0)
    def _():
        s_ref[...] = h0_ref[...]

    def tokens(j):
        if cb == 1:
            return u_ref[:, j, :]
        return u_ref[:, pl.ds(j, cb, stride=tc), :].reshape(rows, LANES)

    us = [tokens(j) for j in range(tc)]
    ucat = jnp.concatenate([u.astype(BF16) for u in us], axis=1)
    e = jnp.dot(ucat, bw_ref[...], preferred_element_type=F32)
    e_ref[...] = e.reshape(nb, cb, S5_TILE_STATE)

    a_re = are_ref[...]
    a_im = aim_ref[...]

    def advance(c, carry):
        for b in range(nb):
            s = s_ref[b]
            sp_ref[b, pl.ds(c, 1), :] = s
            s_ref[b] = a_re * s + a_im * pltpu.roll(s, S5_TILE_STATE // 2, 1) + e_ref[b, pl.ds(c, 1), :]
        return carry

    lax.fori_loop(0, cb, advance, 0)

    sp = sp_ref[...].reshape(rows, S5_TILE_STATE).astype(BF16)
    d = d_ref[...]
    y_state = jnp.dot(sp, bv_ref[...], preferred_element_type=F32)
    pair = 2 * LANES
    for k in range(tc // 2):
        kk = pair * (k + 1)
        yk = jnp.dot(ucat[:, :kk], bd_ref[:kk, k * pair:(k + 1) * pair], preferred_element_type=F32)
        yk = yk + y_state[:, k * pair:(k + 1) * pair]
        for jj in range(2):
            jo = 2 * k + jj
            y = yk[:, jj * LANES:(jj + 1) * LANES] + us[jo] * d
            if cb == 1:
                y_ref[:, jo, :] = y
            else:
                y_ref[:, pl.ds(jo, cb, stride=tc), :] = y.reshape(nb, cb, LANES)

    @pl.when(t == pl.num_programs(1) - 1)
    def _():
        send_ref[...] = s_ref[...]


def _s5(z3, u_col, mats, h0, cb):
    bw, bd, bv, a_re, a_im, d = mats
    nb, l, _ = z3.shape
    tiles = bw.shape[0]
    tc = S5_CHUNK
    blk = tc * cb
    col0 = u_col // LANES
    return pl.pallas_call(
        functools.partial(_s5_kernel, nb=nb, cb=cb),
        grid=(tiles, l // blk),
        in_specs=[
            pl.BlockSpec((nb, blk, LANES), lambda g, t: (0, t, col0 + g)),
            pl.BlockSpec((None, tc * LANES, S5_TILE_STATE), lambda g, t: (g, 0, 0)),
            pl.BlockSpec((None, tc * LANES, tc * LANES), lambda g, t: (g, 0, 0)),
            pl.BlockSpec((None, S5_TILE_STATE, tc * LANES), lambda g, t: (g, 0, 0)),
            pl.BlockSpec((None, 1, S5_TILE_STATE), lambda g, t: (g, 0, 0)),
            pl.BlockSpec((None, 1, S5_TILE_STATE), lambda g, t: (g, 0, 0)),
            pl.BlockSpec((None, 1, LANES), lambda g, t: (g, 0, 0)),
            pl.BlockSpec((nb, None, 1, S5_TILE_STATE), lambda g, t: (0, g, 0, 0)),
        ],
        out_specs=[
            pl.BlockSpec((nb, blk, LANES), lambda g, t: (0, t, g)),
            pl.BlockSpec((nb, None, 1, S5_TILE_STATE), lambda g, t: (0, g, 0, 0)),
        ],
        out_shape=[jax.ShapeDtypeStruct((nb, l, tiles * LANES), F32), jax.ShapeDtypeStruct(h0.shape, F32)],
        scratch_shapes=[
            pltpu.VMEM((nb, 1, S5_TILE_STATE), F32),
            pltpu.VMEM((nb, cb, S5_TILE_STATE), F32),
            pltpu.VMEM((nb, cb, S5_TILE_STATE), F32),
        ],
        compiler_params=_params("arbitrary", "arbitrary"),
        name="s5",
    )(z3, bw, bd, bv, a_re, a_im, d, h0)


def _s5_matrices(lam_re, lam_im, log_dt, b_re, b_im, c_re, c_im, d_skip):
    g, p = lam_re.shape
    tc = S5_CHUNK
    dt = jnp.exp(log_dt)[:, None]
    k = jnp.arange(tc + 1, dtype=F32)[:, None, None]
    mag = jnp.exp(lam_re * dt * k)
    pw_re = mag * jnp.cos(lam_im * dt * k)
    pw_im = mag * jnp.sin(lam_im * dt * k)
    den = lam_re * lam_re + lam_im * lam_im
    nr = pw_re[1] - 1.0
    f_re = (nr * lam_re + pw_im[1] * lam_im) / den
    f_im = (pw_im[1] * lam_re - nr * lam_im) / den
    bb_re = f_re[..., None] * b_re - f_im[..., None] * b_im
    bb_im = f_re[..., None] * b_im + f_im[..., None] * b_re
    ein = functools.partial(jnp.einsum, precision=HIGHEST)
    cp_re = c_re[None] * pw_re[:tc, :, None, :] - c_im[None] * pw_im[:tc, :, None, :]
    cp_im = c_re[None] * pw_im[:tc, :, None, :] + c_im[None] * pw_re[:tc, :, None, :]
    taps = ein('tghp,gpk->tghk', cp_re, bb_re) - ein('tghp,gpk->tghk', cp_im, bb_im)
    k_rev = (tc - 1) - jnp.arange(tc, dtype=F32)[:, None, None]
    mag_rev = jnp.exp(lam_re * dt * k_rev)
    rp_re = (mag_rev * jnp.cos(lam_im * dt * k_rev))[:, :, :, None]
    rp_im = (mag_rev * jnp.sin(lam_im * dt * k_rev))[:, :, :, None]
    w_re = rp_re * bb_re[None] - rp_im * bb_im[None]
    w_im = rp_re * bb_im[None] + rp_im * bb_re[None]
    z_re = c_re[None] * pw_re[1:, :, None, :] - c_im[None] * pw_im[1:, :, None, :]
    z_im = c_re[None] * pw_im[1:, :, None, :] + c_im[None] * pw_re[1:, :, None, :]
    gl = S5_TILE_GROUPS
    tiles = g // gl
    h = S5_GROUP
    tok_lane = jnp.arange(tc * LANES)
    st_lane = jnp.arange(S5_TILE_STATE)
    tok_group = (tok_lane // h) % gl
    st_group = (st_lane // p) % gl
    cmp_lane = jnp.arange(2 * p)
    rep_state = ((st_lane[None, :] // (gl * p) == cmp_lane[:, None] // p)
                 & (st_lane[None, :] % p == cmp_lane[:, None] % p)).astype(BF16)

    def expand(values, rep, row_group, col_group, rep_rows=False):
        spec = 'rq,tqc->trc' if rep_rows else 'trq,qc->trc'
        args = (rep.T, values.astype(BF16)) if rep_rows else (values.astype(BF16), rep)
        wide = jnp.einsum(spec, *args, preferred_element_type=F32)
        return jnp.where(row_group[:, None] == col_group[None, :], wide, 0.0).astype(BF16)

    tp = taps.reshape(tc, tiles, gl, h, h).transpose(1, 0, 2, 4, 3)
    eye = jnp.eye(gl, dtype=F32)
    bd = (tp[:, :, :, :, None, :] * eye[None, None, :, None, :, None]).reshape(tiles, tc, LANES, LANES).astype(BF16)
    lag = jnp.arange(tc)[None, :] - jnp.arange(tc)[:, None]
    bd = jnp.where((lag >= 0)[None, :, :, None, None], bd[:, jnp.clip(lag, 0, tc - 1)], 0)
    bd = bd.transpose(0, 1, 3, 2, 4).reshape(tiles, tc * LANES, tc * LANES)
    ws = jnp.stack([w_re, w_im]).reshape(2, tc, tiles, gl, p, h).transpose(2, 1, 3, 5, 0, 4)
    bw = expand(ws.reshape(tiles, tc * LANES, 2 * p), rep_state, tok_group, st_group)
    zs = jnp.stack([z_re, -z_im]).reshape(2, tc, tiles, gl, h, p).transpose(2, 0, 5, 1, 3, 4)
    bv = expand(zs.reshape(tiles, 2 * p, tc * LANES), rep_state, st_group, tok_group, rep_rows=True)
    ar = pw_re[tc].reshape(tiles, gl * p)
    ai = pw_im[tc].reshape(tiles, gl * p)
    a_re = jnp.concatenate([ar, ar], axis=-1)[:, None, :]
    a_im = jnp.concatenate([-ai, ai], axis=-1)[:, None, :]
    d = d_skip.reshape(tiles, 1, LANES)
    return bw.astype(BF16), bd.astype(BF16), bv.astype(BF16), a_re, a_im, d


def _s5_pack_state(re, im):
    b, g, p = re.shape
    tiles = g // S5_TILE_GROUPS
    return jnp.concatenate([re.reshape(b, tiles, 1, -1), im.reshape(b, tiles, 1, -1)], axis=-1)


def _s5_unpack_state(s, g):
    b = s.shape[0]
    half = S5_TILE_STATE // 2
    return s[..., :half].reshape(b, g, S5_STATE), s[..., half:].reshape(b, g, S5_STATE)


def _mix_kernel(o_ref, y_ref, ga_ref, gb_ref, wa_ref, wv_ref, wg_ref, out_ref, g5_ref):
    @pl.when(pl.program_id(1) == 0)
    def _():
        g5_ref[...] = _gelu_tanh(y_ref[...].astype(F32)).astype(BF16)

    ya = jnp.dot(o_ref[...], wa_ref[...], preferred_element_type=F32)
    g5 = g5_ref[...]
    yb = (jnp.dot(g5, wv_ref[...], preferred_element_type=F32)
          * _sigmoid(jnp.dot(g5, wg_ref[...], preferred_element_type=F32)))
    out = _sigmoid(ga_ref[...]) * ya + _sigmoid(gb_ref[...]) * yb
    out_ref[...] = out.astype(out_ref.dtype)


def _mix(o, y, z, gate_col, wa, wv, wg, tm, tn):
    t, kdim = o.shape
    n = wa.shape[1]
    ga0 = gate_col // tn
    gb0 = (gate_col + n) // tn
    return pl.pallas_call(
        _mix_kernel,
        grid=(t // tm, n // tn),
        in_specs=[
            pl.BlockSpec((tm, kdim), lambda i, j: (i, 0)),
            pl.BlockSpec((tm, kdim), lambda i, j: (i, 0)),
            pl.BlockSpec((tm, tn), lambda i, j: (i, ga0 + j)),
            pl.BlockSpec((tm, tn), lambda i, j: (i, gb0 + j)),
            pl.BlockSpec((kdim, tn), lambda i, j: (0, j)),
            pl.BlockSpec((kdim, tn), lambda i, j: (0, j)),
            pl.BlockSpec((kdim, tn), lambda i, j: (0, j)),
        ],
        out_specs=pl.BlockSpec((tm, tn), lambda i, j: (i, j)),
        out_shape=jax.ShapeDtypeStruct((t, n), BF16),
        scratch_shapes=[pltpu.VMEM((tm, kdim), BF16)],
        compiler_params=_params("arbitrary", "arbitrary"),
        name="mix",
    )(o, y, z, z, wa, wv, wg)


def _mm_norm_kernel(a_ref, w_ref, x_ref, g_ref, out_ref, *, k_steps):
    def product():
        return jnp.dot(a_ref[...], w_ref[...], preferred_element_type=F32)

    def finish(m):
        out_ref[...] = x_ref[...] + (m * _rms_scale(m)) * g_ref[...]

    if k_steps == 1:
        finish(product())
        return
    k = pl.program_id(1)

    @pl.when(k == 0)
    def _():
        out_ref[...] = product()

    @pl.when((k > 0) & (k < k_steps - 1))
    def _():
        out_ref[...] += product()

    @pl.when(k == k_steps - 1)
    def _():
        finish(out_ref[...] + product())


def _mm_norm(a, w, x, g, tm, tk):
    t, kdim = a.shape
    n = w.shape[1]
    return pl.pallas_call(
        functools.partial(_mm_norm_kernel, k_steps=kdim // tk),
        grid=(t // tm, kdim // tk),
        in_specs=[
            pl.BlockSpec((tm, tk), lambda i, k: (i, k)),
            pl.BlockSpec((tk, n), lambda i, k: (k, 0)),
            pl.BlockSpec((tm, n), lambda i, k: (i, 0)),
            pl.BlockSpec((1, n), lambda i, k: (0, 0)),
        ],
        out_specs=pl.BlockSpec((tm, n), lambda i, k: (i, 0)),
        out_shape=jax.ShapeDtypeStruct((t, n), F32),
        compiler_params=_params("arbitrary", "arbitrary"),
        name="mm_norm",
    )(a, w, x, g)


def _ffn_up_kernel(x_ref, g_ref, wa_ref, wb_ref, cw_ref, buf_ref, act_ref, nbuf_ref, h_ref, tail_ref,
                   *, rows_outer, blocks_per_seq):
    if rows_outer:
        i, j = pl.program_id(0), pl.program_id(1)
        fresh_rows = j == 0
    else:
        j, i = pl.program_id(0), pl.program_id(1)
        fresh_rows = True

    def normalise():
        x = x_ref[...]
        h_ref[...] = ((x * _rms_scale(x)) * g_ref[...]).astype(BF16)

    if rows_outer:
        pl.when(fresh_rows)(normalise)
    else:
        normalise()

    if blocks_per_seq > 1:
        @pl.when((i == 0) & (j == 0))
        def _():
            tail_ref[...] = jnp.zeros_like(tail_ref)

    h = h_ref[...]
    a = jnp.dot(h, wa_ref[...], preferred_element_type=F32)
    b = jnp.dot(h, wb_ref[...], preferred_element_type=F32)
    tm = a.shape[0]
    if blocks_per_seq == 1:
        prev = buf_ref[...]
    else:
        prev = jnp.where(i % blocks_per_seq == 0, buf_ref[...], tail_ref[j])
    last = a[tm - SUBLANES:, :]
    tail_ref[j] = last
    nbuf_ref[...] = last
    row = lax.broadcasted_iota(jnp.int32, a.shape, 0)
    p1 = prev[SUBLANES - 1:SUBLANES, :]
    p2 = prev[SUBLANES - 2:SUBLANES - 1, :]
    a1 = jnp.where(row == 0, p1, pltpu.roll(a, 1, 0))
    a2 = jnp.where(row == 0, p2, jnp.where(row == 1, p1, pltpu.roll(a, 2, 0)))
    cw = cw_ref[...]
    conv = a2 * cw[0:1, :] + a1 * cw[1:2, :] + a * cw[2:3, :]
    act_ref[...] = (_gelu_tanh(conv) * b).astype(act_ref.dtype)


def _ffn_up(x, g, w_up, cw, buf, seq_len, tm, tn, rows_outer):
    t, d = x.shape
    f = w_up.shape[1] // 2
    nj = f // tn
    bps = seq_len // tm
    if rows_outer:
        grid = (t // tm, nj)
        ij = lambda a, b: (a, b)
    else:
        grid = (nj, t // tm)
        ij = lambda a, b: (b, a)

    def spec(shape, fn):
        return pl.BlockSpec(shape, lambda a, b: fn(*ij(a, b)))

    return pl.pallas_call(
        functools.partial(_ffn_up_kernel, rows_outer=rows_outer, blocks_per_seq=bps),
        grid=grid,
        in_specs=[
            spec((tm, d), lambda i, j: (i, 0)),
            spec((1, d), lambda i, j: (0, 0)),
            spec((d, tn), lambda i, j: (0, j)),
            spec((d, tn), lambda i, j: (0, nj + j)),
            spec((SUBLANES, tn), lambda i, j: (0, j)),
            spec((None, SUBLANES, tn), lambda i, j: (i // bps, 0, j)),
        ],
        out_specs=[
            spec((tm, tn), lambda i, j: (i, j)),
            spec((None, SUBLANES, tn), lambda i, j: (i, 0, j)),
        ],
        out_shape=[jax.ShapeDtypeStruct((t, f), BF16), jax.ShapeDtypeStruct((t // tm, SUBLANES, f), F32)],
        scratch_shapes=[pltpu.VMEM((tm, d), BF16), pltpu.VMEM((nj, SUBLANES, tn), F32)],
        compiler_params=_params("arbitrary", "arbitrary"),
        name="ffn_up",
    )(x, g, w_up, w_up, cw, buf)


def _pick(t, pref):
    return pref if t % pref == 0 else t


def _tile_plan(t, l, d, f):
    tm = _pick(t, 1024)
    long_seq = l % tm == 0
    return dict(
        tm=tm,
        tn=512,
        tq=1024,
        long_seq=long_seq,
        tm_up=tm if long_seq else l,
        s5_chunks=min(64, l // S5_CHUNK),
        tm_mm=_pick(t, 512),
        tk_out=d,
        tk_down=f // 4 if (f // 4) % LANES == 0 else 512,
    )


def _trunk(x, weights, caches):
    b, l, d = x.shape
    t = b * l
    depth = weights['w_main'].shape[0]
    n_main = weights['w_main'].shape[2]
    gate_col = n_main - 2 * d
    u_col = 3 * ATT_W
    g_s5 = weights['lam_re'].shape[1]
    f = weights['w_down'].shape[1]
    tp = _tile_plan(t, l, d, f)
    tm, tn, tm_up, long_seq = tp['tm'], tp['tn'], tp['tm_up'], tp['long_seq']
    hp = ATT_W // LANES
    x = x.reshape(t, d)
    st = {k: [] for k in ('k', 'v', 'logf', 're', 'im', 'conv')}
    for layer in range(depth):
        wl = {k: v[layer] for k, v in weights.items()}
        z, logf = _norm_in(x, wl['g_mix_pre'], wl['w_main'], wl['w_f'], wl['b_f'], tm, tn)
        st['logf'].append(logf[:, :N_HEADS].reshape(b, l, N_HEADS))
        z3 = z.reshape(b, l, n_main)
        if caches is None:
            c = _cumsum_rows(logf.reshape(b, l, LANES), CUMSUM_ROWS)
            o, kt, vt = _attn_prompt(z3, c, tp['tq'])
            st['k'].append(kt.reshape(b, N_HEADS, HEAD_DIM, l))
            st['v'].append(vt.reshape(b, N_HEADS, HEAD_DIM, l))
            h0 = jnp.zeros((b, g_s5 // S5_TILE_GROUPS, 1, S5_TILE_STATE), F32)
            buf = jnp.zeros((b, SUBLANES, f), F32)
        else:
            st['k'].append(z[:, ATT_W:2 * ATT_W].reshape(b, l, N_HEADS, HEAD_DIM))
            st['v'].append(z[:, 2 * ATT_W:3 * ATT_W].reshape(b, l, N_HEADS, HEAD_DIM))
            past = caches['logf'].shape[2]
            lf_c = jnp.pad(caches['logf'][layer], ((0, 0), (0, 0), (0, LANES - N_HEADS)))
            lf_all = jnp.concatenate([lf_c, logf.reshape(b, l, LANES)], axis=1)
            padded = -(-(past + l) // CUMSUM_ROWS) * CUMSUM_ROWS
            lf_all = jnp.pad(lf_all, ((0, 0), (0, padded - past - l), (0, 0)))
            nck_all = -_cumsum_rows(lf_all, CUMSUM_ROWS)[:, :, :N_HEADS].transpose(0, 2, 1)
            nck_c = nck_all[:, :, :past].reshape(b, hp, 2, past)
            nck_n = nck_all[:, :, past:past + l].reshape(b, hp, 2, l)
            o = _attn_sample(z3, caches['k'], caches['v'], layer, nck_c, nck_n)
            h0 = _s5_pack_state(caches['ssm_re'][layer], caches['ssm_im'][layer])
            buf = jnp.pad(caches['conv'][layer], ((0, 0), (SUBLANES - (CONV_W - 1), 0), (0, 0)))
        o = o.reshape(t, ATT_W)
        y, s_end = _s5(z3, u_col, weights['s5_mats'][layer], h0, tp['s5_chunks'])
        y = y.reshape(t, g_s5 * S5_GROUP)
        s_re, s_im = _s5_unpack_state(s_end, g_s5)
        st['re'].append(s_re)
        st['im'].append(s_im)
        mix_in = _mix(o, y, z, gate_col, wl['w_att_proj'], wl['w_glu_v'], wl['w_glu_g'], tm, tn)
        x = _mm_norm(mix_in, wl['w_out'], x, wl['g_mix_post'], tp['tm_mm'], tp['tk_out'])
        act, nbuf = _ffn_up(x, wl['g_ffn_pre'], wl['w_up'], wl['conv_w'], buf, l, tm_up, tn, long_seq)
        nbuf = nbuf.reshape(b, l // tm_up, SUBLANES, f)[:, -1]
        st['conv'].append(nbuf[:, SUBLANES - (CONV_W - 1):])
        x = _mm_norm(act, wl['w_down'], x, wl['g_ffn_post'], tp['tm_mm'], tp['tk_down'])
    stacked = [jnp.stack(st[k]) for k in ('k', 'v', 'logf', 're', 'im', 'conv')]
    if caches is None:
        stacked[0] = stacked[0].transpose(0, 1, 4, 2, 3)
        stacked[1] = stacked[1].transpose(0, 1, 4, 2, 3)
    return x.reshape(b, l, d), stacked


def kernel(x_prompt, x_sample, cache_k, cache_v, cache_logf, state_ssm_re, state_ssm_im, state_conv,
           g_mix_pre, w_in, b_f, lam_re, lam_im, log_dt, b_re, b_im, c_re, c_im, d_skip,
           w_att_proj, w_glu_v, w_glu_g, w_out, g_mix_post, g_ffn_pre, w_up, conv_w, w_down, g_ffn_post):
    depth, d, _ = w_in.shape
    s5_w = d_skip.shape[1]
    f_lo, f_hi = 3 * ATT_W, 3 * ATT_W + N_HEADS
    scale = HEAD_DIM ** -0.5
    w_main = jnp.concatenate([w_in[:, :, :ATT_W] * scale, w_in[:, :, ATT_W:f_lo], w_in[:, :, f_hi:]], axis=2)
    weights = {
        'w_main': w_main.astype(BF16),
        'w_f': jnp.pad(w_in[:, :, f_lo:f_hi], ((0, 0), (0, 0), (0, LANES - N_HEADS))).astype(BF16),
        'b_f': jnp.pad(b_f, ((0, 0), (0, LANES - N_HEADS))).reshape(depth, 1, LANES),
        'g_mix_pre': g_mix_pre.reshape(depth, 1, d),
        'g_mix_post': g_mix_post.reshape(depth, 1, d),
        'g_ffn_pre': g_ffn_pre.reshape(depth, 1, d),
        'g_ffn_post': g_ffn_post.reshape(depth, 1, d),
        'lam_re': lam_re,
        's5_mats': [_s5_matrices(lam_re[n], lam_im[n], log_dt[n], b_re[n], b_im[n], c_re[n], c_im[n], d_skip[n])
                    for n in range(depth)],
        'w_att_proj': w_att_proj.astype(BF16),
        'w_glu_v': w_glu_v.astype(BF16),
        'w_glu_g': w_glu_g.astype(BF16),
        'w_out': w_out.astype(BF16),
        'w_up': w_up.astype(BF16),
        'conv_w': jnp.pad(conv_w, ((0, 0), (0, SUBLANES - CONV_W), (0, 0))),
        'w_down': w_down.astype(BF16),
    }
    assert s5_w == lam_re.shape[1] * S5_GROUP
    sb, sp = cache_k.shape[1], cache_k.shape[2]
    caches = {
        'k': cache_k.transpose(0, 1, 3, 4, 2).reshape(depth, sb, ATT_W, sp),
        'v': cache_v.transpose(0, 1, 3, 4, 2).reshape(depth, sb, ATT_W, sp),
        'logf': cache_logf, 'ssm_re': state_ssm_re, 'ssm_im': state_ssm_im, 'conv': state_conv,
    }
    y_prompt, p_st = _trunk(x_prompt, weights, None)
    y_sample, s_st = _trunk(x_sample, weights, caches)
    return (y_prompt, y_sample, *p_st, *s_st)
```

```python
import functools

import jax
import jax.numpy as jnp
from jax import lax
from jax.experimental import pallas as pl
from jax.experimental.pallas import tpu as pltpu

F32 = jnp.float32
BF16 = jnp.bfloat16
HIGHEST = lax.Precision.HIGHEST

RMS_EPS = 1e-6
N_HEADS = 16
HEAD_DIM = 64
ATT_W = N_HEADS * HEAD_DIM
S5_GROUP = 16
S5_STATE = 64
CONV_W = 3
LANES = 128
SUBLANES = 8
S5_CHUNK = 16
VMEM_LIMIT_BYTES = 56 * 1024 * 1024


def _params(*sem):
    return pltpu.CompilerParams(dimension_semantics=sem, vmem_limit_bytes=VMEM_LIMIT_BYTES)


def _gelu_tanh(x):
    return 0.5 * x * (1.0 + jnp.tanh(0.7978845608028654 * (x + 0.044715 * (x * x * x))))


def _sigmoid(x):
    return 0.5 * jnp.tanh(0.5 * x) + 0.5


def _log_sigmoid(x):
    return jnp.minimum(x, 0.0) - jnp.log1p(jnp.exp(-jnp.abs(x)))


def _rms_scale(x):
    return lax.rsqrt(jnp.mean(x * x, axis=-1, keepdims=True) + RMS_EPS)


def _norm_in_kernel(x_ref, g_ref, w_ref, wf_ref, bf_ref, z_ref, lf_ref, h_ref):
    @pl.when(pl.program_id(1) == 0)
    def _():
        x = x_ref[...]
        h = ((x * _rms_scale(x)) * g_ref[...]).astype(BF16)
        h_ref[...] = h
        fl = jnp.dot(h, wf_ref[...], preferred_element_type=F32) + bf_ref[...]
        lf_ref[...] = _log_sigmoid(fl)

    z_ref[...] = jnp.dot(h_ref[...], w_ref[...], preferred_element_type=F32)


def _norm_in(x, g, w, wf, bf, tm, tn):
    t, d = x.shape
    n = w.shape[1]
    return pl.pallas_call(
        _norm_in_kernel,
        grid=(t // tm, n // tn),
        in_specs=[
            pl.BlockSpec((tm, d), lambda i, j: (i, 0)),
            pl.BlockSpec((1, d), lambda i, j: (0, 0)),
            pl.BlockSpec((d, tn), lambda i, j: (0, j)),
            pl.BlockSpec((d, LANES), lambda i, j: (0, 0)),
            pl.BlockSpec((1, LANES), lambda i, j: (0, 0)),
        ],
        out_specs=[
            pl.BlockSpec((tm, tn), lambda i, j: (i, j)),
            pl.BlockSpec((tm, LANES), lambda i, j: (i, 0)),
        ],
        out_shape=[jax.ShapeDtypeStruct((t, n), F32), jax.ShapeDtypeStruct((t, LANES), F32)],
        scratch_shapes=[pltpu.VMEM((tm, d), BF16)],
        compiler_params=_params("arbitrary", "arbitrary"),
        name="norm_in",
    )(x, g, w, wf, bf)


CUMSUM_ROWS = 512


def _cumsum_rows_kernel(x_ref, o_ref, carry_ref):
    @pl.when(pl.program_id(1) == 0)
    def _():
        carry_ref[...] = jnp.zeros_like(carry_ref)

    n = x_ref.shape[0]
    row = lax.broadcasted_iota(jnp.int32, (n, n), 0)
    col = lax.broadcasted_iota(jnp.int32, (n, n), 1)
    lower = (col <= row).astype(F32)
    c = jnp.dot(lower, x_ref[...], precision=HIGHEST, preferred_element_type=F32) + carry_ref[...]
    o_ref[...] = c
    carry_ref[...] = c[n - 1:n, :]


def _cumsum_rows(x, blk):
    b, l, w = x.shape
    return pl.pallas_call(
        _cumsum_rows_kernel,
        grid=(b, l // blk),
        in_specs=[pl.BlockSpec((None, blk, w), lambda i, j: (i, j, 0))],
        out_specs=pl.BlockSpec((None, blk, w), lambda i, j: (i, j, 0)),
        out_shape=jax.ShapeDtypeStruct((b, l, w), F32),
        scratch_shapes=[pltpu.VMEM((1, w), F32)],
        compiler_params=_params("arbitrary", "arbitrary"),
        name="cumsum_rows",
    )(x)


def _head_masks():
    lane = lax.broadcasted_iota(jnp.int32, (1, LANES), 1)
    return lane < HEAD_DIM, lane >= HEAD_DIM


BIAS_PARTS = 3
V_ROWS = HEAD_DIM + 16

def _attn_kernel(q_ref, k_ref, v_ref, c_ref, kt_all_ref, vt_all_ref, o_ref, kt_ref, vto_ref,
                 kx_ref, vt_ref, m_ref, acc_ref, sa_ref, sb_ref, *, tq, tk):
    del kt_all_ref, vt_all_ref
    hp = pl.program_id(1)
    i = pl.program_id(2)
    n_blk = k_ref.shape[0] // tk
    lane = lax.broadcasted_iota(jnp.int32, (1, LANES), 1)
    own = [(lane >= h * HEAD_DIM) & (lane < (h + 1) * HEAD_DIM) for h in range(2)]
    spare = [(1 - h) * HEAD_DIM for h in range(2)]

    @pl.when(i == 0)
    def _():
        er = lax.broadcasted_iota(jnp.int32, (LANES, LANES), 0)
        ec = lax.broadcasted_iota(jnp.int32, (LANES, LANES), 1)
        row = lax.broadcasted_iota(jnp.int32, (LANES, tk), 0)

        def prep(j, carry):
            r0 = pl.multiple_of(j * tk, tk)
            kblk = k_ref[pl.ds(r0, tk), :]
            nc = -c_ref[pl.ds(r0, tk), :]
            parts = []
            rem = nc
            for _ in range(BIAS_PARTS):
                piece = rem.astype(BF16)
                parts.append(piece)
                rem = rem - piece.astype(F32)
            vt = v_ref[pl.ds(r0, tk), :].T
            for h in range(2):
                bias = jnp.zeros((tk, LANES), F32)
                for n, piece in enumerate(parts):
                    sel = ((er == 2 * hp + h) & (ec == spare[h] + n)).astype(BF16)
                    bias = bias + jnp.dot(piece, sel, preferred_element_type=F32)
                kx_ref[h, pl.ds(r0, tk), :] = (jnp.where(own[h], kblk, 0.0) + bias).astype(BF16)
                vth = vt if h == 0 else pltpu.roll(vt, HEAD_DIM, 0)
                vt_ref[h, j] = jnp.where(row < HEAD_DIM, vth, 1.0)[:V_ROWS].astype(BF16)
            return carry

        lax.fori_loop(0, n_blk, prep, 0)

    own_rows = pl.ds(pl.multiple_of(i * tq, tq), tq)
    kt_ref[...] = k_ref[own_rows, :].T
    vto_ref[...] = v_ref[own_rows, :].T

    q = q_ref[...]
    qx = []
    for h in range(2):
        ones = (lane >= spare[h]) & (lane < spare[h] + BIAS_PARTS)
        qx.append((jnp.where(own[h], q, 0.0) + jnp.where(ones, 1.0, 0.0)).astype(BF16))
    m_ref[...] = jnp.full_like(m_ref, -jnp.inf)
    acc_ref[...] = jnp.zeros_like(acc_ref)
    nt_dims = (((1,), (1,)), ((), ()))

    def scores(kb, s_ref, q_from=0):
        start = pl.multiple_of(kb * tk, tk)
        for h in range(2):
            s_ref[h, :, q_from:] = lax.dot_general(kx_ref[h, pl.ds(start, tk), :], qx[h][q_from:], nt_dims,
                                                   preferred_element_type=F32)

    def absorb(kb, s_ref, first_key=None, q_from=0):
        nq = tq - q_from
        sts = [s_ref[h, :, q_from:] for h in range(2)]
        if first_key is not None:
            key = lax.broadcasted_iota(jnp.int32, (tk, nq), 0) + first_key
            qry = lax.broadcasted_iota(jnp.int32, (tk, nq), 1) + q_from
            sts = [jnp.where(key <= qry, st, -jnp.inf) for st in sts]
        m_old = [m_ref[h, :, q_from:] for h in range(2)]
        m_new = [jnp.maximum(m_old[h], jnp.max(sts[h], axis=0, keepdims=True)) for h in range(2)]
        pts = [jnp.exp(sts[h] - m_new[h]).astype(BF16) for h in range(2)]
        pvs = [jnp.dot(vt_ref[h, kb], pts[h], preferred_element_type=F32) for h in range(2)]
        for h in range(2):
            acc_ref[h, :, q_from:] = jnp.exp(m_old[h] - m_new[h]) * acc_ref[h, :, q_from:] + pvs[h]
            m_ref[h, :, q_from:] = m_new[h]

    scores(0, sa_ref)

    def pair(p, carry):
        kb = 2 * p
        scores(kb + 1, sb_ref)
        absorb(kb, sa_ref)
        scores(kb + 2, sa_ref)
        absorb(kb + 1, sb_ref)
        return carry

    lax.fori_loop(0, i, pair, 0)
    scores(2 * i + 1, sb_ref, q_from=tk)
    absorb(2 * i, sa_ref, first_key=0)
    absorb(2 * i + 1, sb_ref, first_key=tk, q_from=tk)

    outs = []
    for h in range(2):
        a = acc_ref[h]
        outs.append(a[:HEAD_DIM] * (1.0 / a[HEAD_DIM:HEAD_DIM + 1]))
    o_ref[...] = jnp.concatenate(outs, axis=0).T.astype(o_ref.dtype)


def _attn_prompt(z, c, tq, kt_all, vt_all, layer):
    b, l, _ = z.shape
    hp = ATT_W // LANES
    tk = tq // 2
    return pl.pallas_call(
        functools.partial(_attn_kernel, tq=tq, tk=tk),
        grid=(b, hp, l // tq),
        in_specs=[
            pl.BlockSpec((None, tq, LANES), lambda bi, h, i: (bi, i, h)),
            pl.BlockSpec((None, l, LANES), lambda bi, h, i: (bi, 0, hp + h)),
            pl.BlockSpec((None, l, LANES), lambda bi, h, i: (bi, 0, 2 * hp + h)),
            pl.BlockSpec((None, l, LANES), lambda bi, h, i: (bi, 0, 0)),
            pl.BlockSpec(memory_space=pl.ANY),
            pl.BlockSpec(memory_space=pl.ANY),
        ],
        out_specs=[
            pl.BlockSpec((None, tq, LANES), lambda bi, h, i: (bi, i, h)),
            pl.BlockSpec((None, None, LANES, tq), lambda bi, h, i: (layer, bi, h, i)),
            pl.BlockSpec((None, None, LANES, tq), lambda bi, h, i: (layer, bi, h, i)),
        ],
        out_shape=[
            jax.ShapeDtypeStruct((b, l, ATT_W), BF16),
            jax.ShapeDtypeStruct(kt_all.shape, F32),
            jax.ShapeDtypeStruct(vt_all.shape, F32),
        ],
        input_output_aliases={4: 1, 5: 2},
        scratch_shapes=[
            pltpu.VMEM((2, l, LANES), BF16),
            pltpu.VMEM((2, l // tk, V_ROWS, tk), BF16),
            pltpu.VMEM((2, 1, tq), F32),
            pltpu.VMEM((2, V_ROWS, tq), F32),
            pltpu.VMEM((2, tk, tq), F32),
            pltpu.VMEM((2, tk, tq), F32),
        ],
        compiler_params=_params("arbitrary", "arbitrary", "arbitrary"),
        name="attn_prompt",
    )(z, z, z, c, kt_all, vt_all)


def _attn_cache_kernel(q_ref, kn_ref, vn_ref, kc_ref, vc_ref, nckc_ref, nckn_ref, o_ref):
    q = q_ref[...]
    lq = q.shape[0]
    masks = _head_masks()
    kc = kc_ref[...].astype(BF16)
    vc = vc_ref[...].astype(BF16)
    kn = kn_ref[...].astype(BF16)
    vn = vn_ref[...].astype(BF16)
    nt_dims = (((1,), (1,)), ((), ()))
    r = lax.broadcasted_iota(jnp.int32, (lq, lq), 0)
    c = lax.broadcasted_iota(jnp.int32, (lq, lq), 1)
    outs = []
    for h in range(2):
        qh = jnp.where(masks[h], q, 0.0).astype(BF16)
        sc = jnp.dot(qh, kc, preferred_element_type=F32) + nckc_ref[pl.ds(h, 1), :]
        sn = lax.dot_general(qh, kn, nt_dims, preferred_element_type=F32) + nckn_ref[pl.ds(h, 1), :]
        sn = jnp.where(c <= r, sn, -jnp.inf)
        m = jnp.maximum(jnp.max(sc, axis=-1, keepdims=True), jnp.max(sn, axis=-1, keepdims=True))
        pc = jnp.exp(sc - m)
        pn = jnp.exp(sn - m)
        den = jnp.sum(pc, axis=-1, keepdims=True) + jnp.sum(pn, axis=-1, keepdims=True)
        acc = (lax.dot_general(pc.astype(BF16), vc, nt_dims, preferred_element_type=F32)
               + jnp.dot(pn.astype(BF16), vn, preferred_element_type=F32))
        outs.append(acc / den)
    o_ref[...] = jnp.where(masks[0], outs[0], outs[1]).astype(o_ref.dtype)


def _attn_sample(z, cache_k, cache_v, layer, nck_c, nck_n):
    b, lq, _ = z.shape
    p = cache_k.shape[3]
    hp = ATT_W // LANES
    return pl.pallas_call(
        _attn_cache_kernel,
        grid=(b, hp),
        in_specs=[
            pl.BlockSpec((None, lq, LANES), lambda bi, h: (bi, 0, h)),
            pl.BlockSpec((None, lq, LANES), lambda bi, h: (bi, 0, hp + h)),
            pl.BlockSpec((None, lq, LANES), lambda bi, h: (bi, 0, 2 * hp + h)),
            pl.BlockSpec((None, None, LANES, p), lambda bi, h: (layer, bi, h, 0)),
            pl.BlockSpec((None, None, LANES, p), lambda bi, h: (layer, bi, h, 0)),
            pl.BlockSpec((None, None, 2, p), lambda bi, h: (bi, h, 0, 0)),
            pl.BlockSpec((None, None, 2, lq), lambda bi, h: (bi, h, 0, 0)),
        ],
        out_specs=pl.BlockSpec((None, lq, LANES), lambda bi, h: (bi, 0, h)),
        out_shape=jax.ShapeDtypeStruct((b, lq, ATT_W), BF16),
        compiler_params=_params("arbitrary", "arbitrary"),
        name="attn_sample",
    )(z, z, z, cache_k, cache_v, nck_c, nck_n)


S5_TILE_GROUPS = LANES // S5_GROUP
S5_TILE_STATE = 2 * S5_TILE_GROUPS * S5_STATE


def _s5_kernel(u_ref, bw_ref, bd_ref, bv_ref, are_ref, aim_ref, d_ref, h0_ref,
               y_ref, send_ref, s_ref, e_ref, sp_ref, *, nb, cb):
    t = pl.program_id(1)
    tc = S5_CHUNK
    rows = nb * cb

    @pl.when(t == 0)
    def _():
        s_ref[...] = h0_ref[...]

    def tokens(j):
        if cb == 1:
            return u_ref[:, j, :]
        return u_ref[:, pl.ds(j, cb, stride=tc), :].reshape(rows, LANES)

    us = [tokens(j) for j in range(tc)]
    ucat = jnp.concatenate([u.astype(BF16) for u in us], axis=1)
    e = jnp.dot(ucat, bw_ref[...], preferred_element_type=F32)
    e_ref[...] = e.reshape(nb, cb, S5_TILE_STATE)

    a_re = are_ref[...]
    a_im = aim_ref[...]

    def advance(c, carry):
        for b in range(nb):
            s = s_ref[b]
            sp_ref[b, pl.ds(c, 1), :] = s
            s_ref[b] = a_re * s + a_im * pltpu.roll(s, S5_TILE_STATE // 2, 1) + e_ref[b, pl.ds(c, 1), :]
        return carry

    lax.fori_loop(0, cb, advance, 0)

    sp = sp_ref[...].reshape(rows, S5_TILE_STATE).astype(BF16)
    d = d_ref[...]
    y_state = jnp.dot(sp, bv_ref[...], preferred_element_type=F32)
    pair = 2 * LANES
    for k in range(tc // 2):
        kk = pair * (k + 1)
        yk = jnp.dot(ucat[:, :kk], bd_ref[:kk, k * pair:(k + 1) * pair], preferred_element_type=F32)
        yk = yk + y_state[:, k * pair:(k + 1) * pair]
        for jj in range(2):
            jo = 2 * k + jj
            y = yk[:, jj * LANES:(jj + 1) * LANES] + us[jo] * d
            if cb == 1:
                y_ref[:, jo, :] = y
            else:
                y_ref[:, pl.ds(jo, cb, stride=tc), :] = y.reshape(nb, cb, LANES)

    @pl.when(t == pl.num_programs(1) - 1)
    def _():
        send_ref[...] = s_ref[...]


def _s5(z3, u_col, mats, h0, cb):
    bw, bd, bv, a_re, a_im, d = mats
    nb, l, _ = z3.shape
    tiles = bw.shape[0]
    tc = S5_CHUNK
    blk = tc * cb
    col0 = u_col // LANES
    return pl.pallas_call(
        functools.partial(_s5_kernel, nb=nb, cb=cb),
        grid=(tiles, l // blk),
        in_specs=[
            pl.BlockSpec((nb, blk, LANES), lambda g, t: (0, t, col0 + g)),
            pl.BlockSpec((None, tc * LANES, S5_TILE_STATE), lambda g, t: (g, 0, 0)),
            pl.BlockSpec((None, tc * LANES, tc * LANES), lambda g, t: (g, 0, 0)),
            pl.BlockSpec((None, S5_TILE_STATE, tc * LANES), lambda g, t: (g, 0, 0)),
            pl.BlockSpec((None, 1, S5_TILE_STATE), lambda g, t: (g, 0, 0)),
            pl.BlockSpec((None, 1, S5_TILE_STATE), lambda g, t: (g, 0, 0)),
            pl.BlockSpec((None, 1, LANES), lambda g, t: (g, 0, 0)),
            pl.BlockSpec((nb, None, 1, S5_TILE_STATE), lambda g, t: (0, g, 0, 0)),
        ],
        out_specs=[
            pl.BlockSpec((nb, blk, LANES), lambda g, t: (0, t, g)),
            pl.BlockSpec((nb, None, 1, S5_TILE_STATE), lambda g, t: (0, g, 0, 0)),
        ],
        out_shape=[jax.ShapeDtypeStruct((nb, l, tiles * LANES), F32), jax.ShapeDtypeStruct(h0.shape, F32)],
        scratch_shapes=[
            pltpu.VMEM((nb, 1, S5_TILE_STATE), F32),
            pltpu.VMEM((nb, cb, S5_TILE_STATE), F32),
            pltpu.VMEM((nb, cb, S5_TILE_STATE), F32),
        ],
        compiler_params=_params("arbitrary", "arbitrary"),
        name="s5",
    )(z3, bw, bd, bv, a_re, a_im, d, h0)


def _s5_matrices(lam_re, lam_im, log_dt, b_re, b_im, c_re, c_im, d_skip):
    g, p = lam_re.shape
    tc = S5_CHUNK
    dt = jnp.exp(log_dt)[:, None]
    k = jnp.arange(tc + 1, dtype=F32)[:, None, None]
    mag = jnp.exp(lam_re * dt * k)
    pw_re = mag * jnp.cos(lam_im * dt * k)
    pw_im = mag * jnp.sin(lam_im * dt * k)
    den = lam_re * lam_re + lam_im * lam_im
    nr = pw_re[1] - 1.0
    f_re = (nr * lam_re + pw_im[1] * lam_im) / den
    f_im = (pw_im[1] * lam_re - nr * lam_im) / den
    bb_re = f_re[..., None] * b_re - f_im[..., None] * b_im
    bb_im = f_re[..., None] * b_im + f_im[..., None] * b_re
    ein = functools.partial(jnp.einsum, precision=HIGHEST)
    cp_re = c_re[None] * pw_re[:tc, :, None, :] - c_im[None] * pw_im[:tc, :, None, :]
    cp_im = c_re[None] * pw_im[:tc, :, None, :] + c_im[None] * pw_re[:tc, :, None, :]
    taps = ein('tghp,gpk->tghk', cp_re, bb_re) - ein('tghp,gpk->tghk', cp_im, bb_im)
    k_rev = (tc - 1) - jnp.arange(tc, dtype=F32)[:, None, None]
    mag_rev = jnp.exp(lam_re * dt * k_rev)
    rp_re = (mag_rev * jnp.cos(lam_im * dt * k_rev))[:, :, :, None]
    rp_im = (mag_rev * jnp.sin(lam_im * dt * k_rev))[:, :, :, None]
    w_re = rp_re * bb_re[None] - rp_im * bb_im[None]
    w_im = rp_re * bb_im[None] + rp_im * bb_re[None]
    z_re = c_re[None] * pw_re[1:, :, None, :] - c_im[None] * pw_im[1:, :, None, :]
    z_im = c_re[None] * pw_im[1:, :, None, :] + c_im[None] * pw_re[1:, :, None, :]
    gl = S5_TILE_GROUPS
    tiles = g // gl
    h = S5_GROUP
    tok_lane = jnp.arange(tc * LANES)
    st_lane = jnp.arange(S5_TILE_STATE)
    tok_group = (tok_lane // h) % gl
    st_group = (st_lane // p) % gl
    cmp_lane = jnp.arange(2 * p)
    rep_state = ((st_lane[None, :] // (gl * p) == cmp_lane[:, None] // p)
                 & (st_lane[None, :] % p == cmp_lane[:, None] % p)).astype(BF16)

    def expand(values, rep, row_group, col_group, rep_rows=False):
        spec = 'rq,tqc->trc' if rep_rows else 'trq,qc->trc'
        args = (rep.T, values.astype(BF16)) if rep_rows else (values.astype(BF16), rep)
        wide = jnp.einsum(spec, *args, preferred_element_type=F32)
        return jnp.where(row_group[:, None] == col_group[None, :], wide, 0.0).astype(BF16)

    tp = taps.reshape(tc, tiles, gl, h, h).transpose(1, 0, 2, 4, 3)
    eye = jnp.eye(gl, dtype=F32)
    bd = (tp[:, :, :, :, None, :] * eye[None, None, :, None, :, None]).reshape(tiles, tc, LANES, LANES).astype(BF16)
    lag = jnp.arange(tc)[None, :] - jnp.arange(tc)[:, None]
    bd = jnp.where((lag >= 0)[None, :, :, None, None], bd[:, jnp.clip(lag, 0, tc - 1)], 0)
    bd = bd.transpose(0, 1, 3, 2, 4).reshape(tiles, tc * LANES, tc * LANES)
    ws = jnp.stack([w_re, w_im]).reshape(2, tc, tiles, gl, p, h).transpose(2, 1, 3, 5, 0, 4)
    bw = expand(ws.reshape(tiles, tc * LANES, 2 * p), rep_state, tok_group, st_group)
    zs = jnp.stack([z_re, -z_im]).reshape(2, tc, tiles, gl, h, p).transpose(2, 0, 5, 1, 3, 4)
    bv = expand(zs.reshape(tiles, 2 * p, tc * LANES), rep_state, st_group, tok_group, rep_rows=True)
    ar = pw_re[tc].reshape(tiles, gl * p)
    ai = pw_im[tc].reshape(tiles, gl * p)
    a_re = jnp.concatenate([ar, ar], axis=-1)[:, None, :]
    a_im = jnp.concatenate([-ai, ai], axis=-1)[:, None, :]
    d = d_skip.reshape(tiles, 1, LANES)
    return bw.astype(BF16), bd.astype(BF16), bv.astype(BF16), a_re, a_im, d


def _s5_pack_state(re, im):
    b, g, p = re.shape
    tiles = g // S5_TILE_GROUPS
    return jnp.concatenate([re.reshape(b, tiles, 1, -1), im.reshape(b, tiles, 1, -1)], axis=-1)


def _s5_unpack_state(s, g):
    b = s.shape[0]
    half = S5_TILE_STATE // 2
    return s[..., :half].reshape(b, g, S5_STATE), s[..., half:].reshape(b, g, S5_STATE)


def _mix_kernel(o_ref, y_ref, ga_ref, gb_ref, wa_ref, wv_ref, wg_ref, out_ref, g5_ref):
    @pl.when(pl.program_id(1) == 0)
    def _():
        g5_ref[...] = _gelu_tanh(y_ref[...].astype(F32)).astype(BF16)

    ya = jnp.dot(o_ref[...], wa_ref[...], preferred_element_type=F32)
    g5 = g5_ref[...]
    yb = (jnp.dot(g5, wv_ref[...], preferred_element_type=F32)
          * _sigmoid(jnp.dot(g5, wg_ref[...], preferred_element_type=F32)))
    out = _sigmoid(ga_ref[...]) * ya + _sigmoid(gb_ref[...]) * yb
    out_ref[...] = out.astype(out_ref.dtype)


def _mix(o, y, z, gate_col, wa, wv, wg, tm, tn):
    t, kdim = o.shape
    n = wa.shape[1]
    ga0 = gate_col // tn
    gb0 = (gate_col + n) // tn
    return pl.pallas_call(
        _mix_kernel,
        grid=(t // tm, n // tn),
        in_specs=[
            pl.BlockSpec((tm, kdim), lambda i, j: (i, 0)),
            pl.BlockSpec((tm, kdim), lambda i, j: (i, 0)),
            pl.BlockSpec((tm, tn), lambda i, j: (i, ga0 + j)),
            pl.BlockSpec((tm, tn), lambda i, j: (i, gb0 + j)),
            pl.BlockSpec((kdim, tn), lambda i, j: (0, j)),
            pl.BlockSpec((kdim, tn), lambda i, j: (0, j)),
            pl.BlockSpec((kdim, tn), lambda i, j: (0, j)),
        ],
        out_specs=pl.BlockSpec((tm, tn), lambda i, j: (i, j)),
        out_shape=jax.ShapeDtypeStruct((t, n), BF16),
        scratch_shapes=[pltpu.VMEM((tm, kdim), BF16)],
        compiler_params=_params("arbitrary", "arbitrary"),
        name="mix",
    )(o, y, z, z, wa, wv, wg)


def _mm_norm_kernel(a_ref, w_ref, x_ref, g_ref, out_ref, *, k_steps):
    def product():
        return jnp.dot(a_ref[...], w_ref[...], preferred_element_type=F32)

    def finish(m):
        out_ref[...] = x_ref[...] + (m * _rms_scale(m)) * g_ref[...]

    if k_steps == 1:
        finish(product())
        return
    k = pl.program_id(1)

    @pl.when(k == 0)
    def _():
        out_ref[...] = product()

    @pl.when((k > 0) & (k < k_steps - 1))
    def _():
        out_ref[...] += product()

    @pl.when(k == k_steps - 1)
    def _():
        finish(out_ref[...] + product())


def _mm_norm(a, w, x, g, tm, tk):
    t, kdim = a.shape
    n = w.shape[1]
    return pl.pallas_call(
        functools.partial(_mm_norm_kernel, k_steps=kdim // tk),
        grid=(t // tm, kdim // tk),
        in_specs=[
            pl.BlockSpec((tm, tk), lambda i, k: (i, k)),
            pl.BlockSpec((tk, n), lambda i, k: (k, 0)),
            pl.BlockSpec((tm, n), lambda i, k: (i, 0)),
            pl.BlockSpec((1, n), lambda i, k: (0, 0)),
        ],
        out_specs=pl.BlockSpec((tm, n), lambda i, k: (i, 0)),
        out_shape=jax.ShapeDtypeStruct((t, n), F32),
        compiler_params=_params("arbitrary", "arbitrary"),
        name="mm_norm",
    )(a, w, x, g)


def _ffn_up_kernel(x_ref, g_ref, wa_ref, wb_ref, cw_ref, buf_ref, act_ref, nbuf_ref, h_ref, tail_ref,
                   *, rows_outer, blocks_per_seq):
    if rows_outer:
        i, j = pl.program_id(0), pl.program_id(1)
        fresh_rows = j == 0
    else:
        j, i = pl.program_id(0), pl.program_id(1)
        fresh_rows = True

    def normalise():
        x = x_ref[...]
        h_ref[...] = ((x * _rms_scale(x)) * g_ref[...]).astype(BF16)

    if rows_outer:
        pl.when(fresh_rows)(normalise)
    else:
        normalise()

    if blocks_per_seq > 1:
        @pl.when((i == 0) & (j == 0))
        def _():
            tail_ref[...] = jnp.zeros_like(tail_ref)

    h = h_ref[...]
    a = jnp.dot(h, wa_ref[...], preferred_element_type=F32)
    b = jnp.dot(h, wb_ref[...], preferred_element_type=F32)
    tm = a.shape[0]
    if blocks_per_seq == 1:
        prev = buf_ref[...]
    else:
        prev = jnp.where(i % blocks_per_seq == 0, buf_ref[...], tail_ref[j])
    last = a[tm - SUBLANES:, :]
    tail_ref[j] = last
    nbuf_ref[...] = last
    row = lax.broadcasted_iota(jnp.int32, a.shape, 0)
    p1 = prev[SUBLANES - 1:SUBLANES, :]
    p2 = prev[SUBLANES - 2:SUBLANES - 1, :]
    a1 = jnp.where(row == 0, p1, pltpu.roll(a, 1, 0))
    a2 = jnp.where(row == 0, p2, jnp.where(row == 1, p1, pltpu.roll(a, 2, 0)))
    cw = cw_ref[...]
    conv = a2 * cw[0:1, :] + a1 * cw[1:2, :] + a * cw[2:3, :]
    act_ref[...] = (_gelu_tanh(conv) * b).astype(act_ref.dtype)


def _ffn_up(x, g, w_up, cw, buf, seq_len, tm, tn, rows_outer):
    t, d = x.shape
    f = w_up.shape[1] // 2
    nj = f // tn
    bps = seq_len // tm
    if rows_outer:
        grid = (t // tm, nj)
        ij = lambda a, b: (a, b)
    else:
        grid = (nj, t // tm)
        ij = lambda a, b: (b, a)

    def spec(shape, fn):
        return pl.BlockSpec(shape, lambda a, b: fn(*ij(a, b)))

    return pl.pallas_call(
        functools.partial(_ffn_up_kernel, rows_outer=rows_outer, blocks_per_seq=bps),
        grid=grid,
        in_specs=[
            spec((tm, d), lambda i, j: (i, 0)),
            spec((1, d), lambda i, j: (0, 0)),
            spec((d, tn), lambda i, j: (0, j)),
            spec((d, tn), lambda i, j: (0, nj + j)),
            spec((SUBLANES, tn), lambda i, j: (0, j)),
            spec((None, SUBLANES, tn), lambda i, j: (i // bps, 0, j)),
        ],
        out_specs=[
            spec((tm, tn), lambda i, j: (i, j)),
            spec((None, SUBLANES, tn), lambda i, j: (i, 0, j)),
        ],
        out_shape=[jax.ShapeDtypeStruct((t, f), BF16), jax.ShapeDtypeStruct((t // tm, SUBLANES, f), F32)],
        scratch_shapes=[pltpu.VMEM((tm, d), BF16), pltpu.VMEM((nj, SUBLANES, tn), F32)],
        compiler_params=_params("arbitrary", "arbitrary"),
        name="ffn_up",
    )(x, g, w_up, w_up, cw, buf)


def _pick(t, pref):
    return pref if t % pref == 0 else t


def _tile_plan(t, l, d, f):
    tm = _pick(t, 1024)
    long_seq = l % tm == 0
    return dict(
        tm=tm,
        tn=512,
        tn_in=1024,
        tq=1024,
        long_seq=long_seq,
        tm_up=tm if long_seq else l,
        s5_chunks=min(64, l // S5_CHUNK),
        tm_mm=_pick(t, 512),
        tk_out=d,
        tk_down=f // 4 if (f // 4) % LANES == 0 else 512,
    )


def _trunk(x, weights, caches):
    b, l, d = x.shape
    t = b * l
    depth = weights['w_main'].shape[0]
    n_main = weights['w_main'].shape[2]
    gate_col = n_main - 2 * d
    u_col = 3 * ATT_W
    g_s5 = weights['lam_re'].shape[1]
    f = weights['w_down'].shape[1]
    tp = _tile_plan(t, l, d, f)
    tm, tn, tm_up, long_seq = tp['tm'], tp['tn'], tp['tm_up'], tp['long_seq']
    hp = ATT_W // LANES
    x = x.reshape(t, d)
    st = {k: [] for k in ('k', 'v', 'logf', 're', 'im', 'conv')}
    if caches is None:
        kt_all = jnp.zeros((depth, b, ATT_W, l), F32)
        vt_all = jnp.zeros((depth, b, ATT_W, l), F32)
    for layer in range(depth):
        wl = {k: v[layer] for k, v in weights.items()}
        z, logf = _norm_in(x, wl['g_mix_pre'], wl['w_main'], wl['w_f'], wl['b_f'], tm, tp['tn_in'])
        st['logf'].append(logf[:, :N_HEADS].reshape(b, l, N_HEADS))
        z3 = z.reshape(b, l, n_main)
        if caches is None:
            c = _cumsum_rows(logf.reshape(b, l, LANES), CUMSUM_ROWS)
            o, kt_all, vt_all = _attn_prompt(z3, c, tp['tq'], kt_all, vt_all, layer)
            h0 = jnp.zeros((b, g_s5 // S5_TILE_GROUPS, 1, S5_TILE_STATE), F32)
            buf = jnp.zeros((b, SUBLANES, f), F32)
        else:
            st['k'].append(z[:, ATT_W:2 * ATT_W].reshape(b, l, N_HEADS, HEAD_DIM))
            st['v'].append(z[:, 2 * ATT_W:3 * ATT_W].reshape(b, l, N_HEADS, HEAD_DIM))
            past = caches['logf'].shape[2]
            lf_c = jnp.pad(caches['logf'][layer], ((0, 0), (0, 0), (0, LANES - N_HEADS)))
            lf_all = jnp.concatenate([lf_c, logf.reshape(b, l, LANES)], axis=1)
            padded = -(-(past + l) // CUMSUM_ROWS) * CUMSUM_ROWS
            lf_all = jnp.pad(lf_all, ((0, 0), (0, padded - past - l), (0, 0)))
            nck_all = -_cumsum_rows(lf_all, CUMSUM_ROWS)[:, :, :N_HEADS].transpose(0, 2, 1)
            nck_c = nck_all[:, :, :past].reshape(b, hp, 2, past)
            nck_n = nck_all[:, :, past:past + l].reshape(b, hp, 2, l)
            o = _attn_sample(z3, caches['k'], caches['v'], layer, nck_c, nck_n)
            h0 = _s5_pack_state(caches['ssm_re'][layer], caches['ssm_im'][layer])
            buf = jnp.pad(caches['conv'][layer], ((0, 0), (SUBLANES - (CONV_W - 1), 0), (0, 0)))
        o = o.reshape(t, ATT_W)
        y, s_end = _s5(z3, u_col, weights['s5_mats'][layer], h0, tp['s5_chunks'])
        y = y.reshape(t, g_s5 * S5_GROUP)
        s_re, s_im = _s5_unpack_state(s_end, g_s5)
        st['re'].append(s_re)
        st['im'].append(s_im)
        mix_in = _mix(o, y, z, gate_col, wl['w_att_proj'], wl['w_glu_v'], wl['w_glu_g'], tm, tn)
        x = _mm_norm(mix_in, wl['w_out'], x, wl['g_mix_post'], tp['tm_mm'], tp['tk_out'])
        act, nbuf = _ffn_up(x, wl['g_ffn_pre'], wl['w_up'], wl['conv_w'], buf, l, tm_up, tn, long_seq)
        nbuf = nbuf.reshape(b, l // tm_up, SUBLANES, f)[:, -1]
        st['conv'].append(nbuf[:, SUBLANES - (CONV_W - 1):])
        x = _mm_norm(act, wl['w_down'], x, wl['g_ffn_post'], tp['tm_mm'], tp['tk_down'])
    stacked = [jnp.stack(st[k]) for k in ('logf', 're', 'im', 'conv')]
    if caches is None:
        kv = [a.reshape(depth, b, N_HEADS, HEAD_DIM, l).transpose(0, 1, 4, 2, 3) for a in (kt_all, vt_all)]
    else:
        kv = [jnp.stack(st['k']), jnp.stack(st['v'])]
    return x.reshape(b, l, d), kv + stacked


def kernel(x_prompt, x_sample, cache_k, cache_v, cache_logf, state_ssm_re, state_ssm_im, state_conv,
           g_mix_pre, w_in, b_f, lam_re, lam_im, log_dt, b_re, b_im, c_re, c_im, d_skip,
           w_att_proj, w_glu_v, w_glu_g, w_out, g_mix_post, g_ffn_pre, w_up, conv_w, w_down, g_ffn_post):
    depth, d, _ = w_in.shape
    s5_w = d_skip.shape[1]
    f_lo, f_hi = 3 * ATT_W, 3 * ATT_W + N_HEADS
    scale = HEAD_DIM ** -0.5
    w_main = jnp.concatenate([w_in[:, :, :ATT_W] * scale, w_in[:, :, ATT_W:f_lo], w_in[:, :, f_hi:]], axis=2)
    weights = {
        'w_main': w_main.astype(BF16),
        'w_f': jnp.pad(w_in[:, :, f_lo:f_hi], ((0, 0), (0, 0), (0, LANES - N_HEADS))).astype(BF16),
        'b_f': jnp.pad(b_f, ((0, 0), (0, LANES - N_HEADS))).reshape(depth, 1, LANES),
        'g_mix_pre': g_mix_pre.reshape(depth, 1, d),
        'g_mix_post': g_mix_post.reshape(depth, 1, d),
        'g_ffn_pre': g_ffn_pre.reshape(depth, 1, d),
        'g_ffn_post': g_ffn_post.reshape(depth, 1, d),
        'lam_re': lam_re,
        's5_mats': [_s5_matrices(lam_re[n], lam_im[n], log_dt[n], b_re[n], b_im[n], c_re[n], c_im[n], d_skip[n])
                    for n in range(depth)],
        'w_att_proj': w_att_proj.astype(BF16),
        'w_glu_v': w_glu_v.astype(BF16),
        'w_glu_g': w_glu_g.astype(BF16),
        'w_out': w_out.astype(BF16),
        'w_up': w_up.astype(BF16),
        'conv_w': jnp.pad(conv_w, ((0, 0), (0, SUBLANES - CONV_W), (0, 0))),
        'w_down': w_down.astype(BF16),
    }
    assert s5_w == lam_re.shape[1] * S5_GROUP
    sb, sp = cache_k.shape[1], cache_k.shape[2]
    caches = {
        'k': cache_k.transpose(0, 1, 3, 4, 2).reshape(depth, sb, ATT_W, sp),
        'v': cache_v.transpose(0, 1, 3, 4, 2).reshape(depth, sb, ATT_W, sp),
        'logf': cache_logf, 'ssm_re': state_ssm_re, 'ssm_im': state_ssm_im, 'conv': state_conv,
    }
    y_prompt, p_st = _trunk(x_prompt, weights, None)
    y_sample, s_st = _trunk(x_sample, weights, caches)
    return (y_prompt, y_sample, *p_st, *s_st)
```

```python
import functools

import jax
import jax.numpy as jnp
from jax import lax
from jax.experimental import pallas as pl
from jax.experimental.pallas import tpu as pltpu

F32 = jnp.float32
BF16 = jnp.bfloat16
HIGHEST = lax.Precision.HIGHEST

RMS_EPS = 1e-6
N_HEADS = 16
HEAD_DIM = 64
ATT_W = N_HEADS * HEAD_DIM
S5_GROUP = 16
S5_STATE = 64
CONV_W = 3
LANES = 128
SUBLANES = 8
S5_CHUNK = 16
VMEM_LIMIT_BYTES = 56 * 1024 * 1024


def _params(*sem):
    return pltpu.CompilerParams(dimension_semantics=sem, vmem_limit_bytes=VMEM_LIMIT_BYTES)


def _gelu_tanh(x):
    return 0.5 * x * (1.0 + jnp.tanh(0.7978845608028654 * (x + 0.044715 * (x * x * x))))


def _sigmoid(x):
    return 0.5 * jnp.tanh(0.5 * x) + 0.5


def _log_sigmoid(x):
    return jnp.minimum(x, 0.0) - jnp.log1p(jnp.exp(-jnp.abs(x)))


def _rms_scale(x):
    return lax.rsqrt(jnp.mean(x * x, axis=-1, keepdims=True) + RMS_EPS)


def _norm_in_kernel(x_ref, g_ref, w_ref, wf_ref, bf_ref, z_ref, lf_ref, h_ref):
    @pl.when(pl.program_id(1) == 0)
    def _():
        x = x_ref[...]
        h = ((x * _rms_scale(x)) * g_ref[...]).astype(BF16)
        h_ref[...] = h
        fl = jnp.dot(h, wf_ref[...], preferred_element_type=F32) + bf_ref[...]
        lf_ref[...] = _log_sigmoid(fl)

    z_ref[...] = jnp.dot(h_ref[...], w_ref[...], preferred_element_type=F32)


def _norm_in(x, g, w, wf, bf, tm, tn):
    t, d = x.shape
    n = w.shape[1]
    return pl.pallas_call(
        _norm_in_kernel,
        grid=(t // tm, n // tn),
        in_specs=[
            pl.BlockSpec((tm, d), lambda i, j: (i, 0)),
            pl.BlockSpec((1, d), lambda i, j: (0, 0)),
            pl.BlockSpec((d, tn), lambda i, j: (0, j)),
            pl.BlockSpec((d, LANES), lambda i, j: (0, 0)),
            pl.BlockSpec((1, LANES), lambda i, j: (0, 0)),
        ],
        out_specs=[
            pl.BlockSpec((tm, tn), lambda i, j: (i, j)),
            pl.BlockSpec((tm, LANES), lambda i, j: (i, 0)),
        ],
        out_shape=[jax.ShapeDtypeStruct((t, n), F32), jax.ShapeDtypeStruct((t, LANES), F32)],
        scratch_shapes=[pltpu.VMEM((tm, d), BF16)],
        compiler_params=_params("arbitrary", "arbitrary"),
        name="norm_in",
    )(x, g, w, wf, bf)


CUMSUM_ROWS = 512


def _cumsum_rows_kernel(x_ref, o_ref, carry_ref):
    @pl.when(pl.program_id(1) == 0)
    def _():
        carry_ref[...] = jnp.zeros_like(carry_ref)

    n = x_ref.shape[0]
    row = lax.broadcasted_iota(jnp.int32, (n, n), 0)
    col = lax.broadcasted_iota(jnp.int32, (n, n), 1)
    lower = (col <= row).astype(F32)
    c = jnp.dot(lower, x_ref[...], precision=HIGHEST, preferred_element_type=F32) + carry_ref[...]
    o_ref[...] = c
    carry_ref[...] = c[n - 1:n, :]


def _lanes_by_sequence(lf, rows):
    b, l, h = lf.shape
    lanes = -(-(b * h) // LANES) * LANES
    x = lf.transpose(1, 0, 2).reshape(l, b * h)
    return jnp.pad(x, ((0, rows - l), (0, lanes - b * h)))[None]


def _cumsum_rows(x, blk):
    b, l, w = x.shape
    return pl.pallas_call(
        _cumsum_rows_kernel,
        grid=(b, l // blk),
        in_specs=[pl.BlockSpec((None, blk, w), lambda i, j: (i, j, 0))],
        out_specs=pl.BlockSpec((None, blk, w), lambda i, j: (i, j, 0)),
        out_shape=jax.ShapeDtypeStruct((b, l, w), F32),
        scratch_shapes=[pltpu.VMEM((1, w), F32)],
        compiler_params=_params("arbitrary", "arbitrary"),
        name="cumsum_rows",
    )(x)


def _head_masks():
    lane = lax.broadcasted_iota(jnp.int32, (1, LANES), 1)
    return lane < HEAD_DIM, lane >= HEAD_DIM


BIAS_PARTS = 3
V_ROWS = HEAD_DIM + 16

def _attn_kernel(q_ref, k_ref, v_ref, c_ref, kt_all_ref, vt_all_ref, o_ref, kt_ref, vto_ref,
                 kx_ref, vt_ref, m_ref, acc_ref, sa_ref, sb_ref, *, tq, tk):
    del kt_all_ref, vt_all_ref
    seq = pl.program_id(0)
    hp = pl.program_id(1)
    i = pl.program_id(2)
    n_blk = k_ref.shape[0] // tk
    lane = lax.broadcasted_iota(jnp.int32, (1, LANES), 1)
    own = [(lane >= h * HEAD_DIM) & (lane < (h + 1) * HEAD_DIM) for h in range(2)]
    spare = [(1 - h) * HEAD_DIM for h in range(2)]

    @pl.when(i == 0)
    def _():
        er = lax.broadcasted_iota(jnp.int32, (LANES, LANES), 0)
        ec = lax.broadcasted_iota(jnp.int32, (LANES, LANES), 1)
        row = lax.broadcasted_iota(jnp.int32, (LANES, tk), 0)

        def prep(j, carry):
            r0 = pl.multiple_of(j * tk, tk)
            kblk = k_ref[pl.ds(r0, tk), :]
            nc = -c_ref[pl.ds(r0, tk), :]
            parts = []
            rem = nc
            for _ in range(BIAS_PARTS):
                piece = rem.astype(BF16)
                parts.append(piece)
                rem = rem - piece.astype(F32)
            vt = v_ref[pl.ds(r0, tk), :].T
            for h in range(2):
                bias = jnp.zeros((tk, LANES), F32)
                for n, piece in enumerate(parts):
                    sel = ((er == seq * N_HEADS + 2 * hp + h) & (ec == spare[h] + n)).astype(BF16)
                    bias = bias + jnp.dot(piece, sel, preferred_element_type=F32)
                kx_ref[h, pl.ds(r0, tk), :] = (jnp.where(own[h], kblk, 0.0) + bias).astype(BF16)
                vth = vt if h == 0 else pltpu.roll(vt, HEAD_DIM, 0)
                vt_ref[h, j] = jnp.where(row < HEAD_DIM, vth, 1.0)[:V_ROWS].astype(BF16)
            return carry

        lax.fori_loop(0, n_blk, prep, 0)

    own_rows = pl.ds(pl.multiple_of(i * tq, tq), tq)
    kt_ref[...] = k_ref[own_rows, :].T
    vto_ref[...] = v_ref[own_rows, :].T

    q = q_ref[...]
    qx = []
    for h in range(2):
        ones = (lane >= spare[h]) & (lane < spare[h] + BIAS_PARTS)
        qx.append((jnp.where(own[h], q, 0.0) + jnp.where(ones, 1.0, 0.0)).astype(BF16))
    m_ref[...] = jnp.full_like(m_ref, -jnp.inf)
    acc_ref[...] = jnp.zeros_like(acc_ref)
    nt_dims = (((1,), (1,)), ((), ()))

    def scores(kb, s_ref, q_from=0):
        start = pl.multiple_of(kb * tk, tk)
        for h in range(2):
            s_ref[h, :, q_from:] = lax.dot_general(kx_ref[h, pl.ds(start, tk), :], qx[h][q_from:], nt_dims,
                                                   preferred_element_type=F32)

    def absorb(kb, s_ref, first_key=None, q_from=0):
        nq = tq - q_from
        sts = [s_ref[h, :, q_from:] for h in range(2)]
        if first_key is not None:
            key = lax.broadcasted_iota(jnp.int32, (tk, nq), 0) + first_key
            qry = lax.broadcasted_iota(jnp.int32, (tk, nq), 1) + q_from
            sts = [jnp.where(key <= qry, st, -jnp.inf) for st in sts]
        m_old = [m_ref[h, :, q_from:] for h in range(2)]
        m_new = [jnp.maximum(m_old[h], jnp.max(sts[h], axis=0, keepdims=True)) for h in range(2)]
        pts = [jnp.exp(sts[h] - m_new[h]).astype(BF16) for h in range(2)]
        pvs = [jnp.dot(vt_ref[h, kb], pts[h], preferred_element_type=F32) for h in range(2)]
        for h in range(2):
            acc_ref[h, :, q_from:] = jnp.exp(m_old[h] - m_new[h]) * acc_ref[h, :, q_from:] + pvs[h]
            m_ref[h, :, q_from:] = m_new[h]

    scores(0, sa_ref)

    def pair(p, carry):
        kb = 2 * p
        scores(kb + 1, sb_ref)
        absorb(kb, sa_ref)
        scores(kb + 2, sa_ref)
        absorb(kb + 1, sb_ref)
        return carry

    lax.fori_loop(0, i, pair, 0)
    scores(2 * i + 1, sb_ref, q_from=tk)
    absorb(2 * i, sa_ref, first_key=0)
    absorb(2 * i + 1, sb_ref, first_key=tk, q_from=tk)

    outs = []
    for h in range(2):
        a = acc_ref[h]
        outs.append(a[:HEAD_DIM] * (1.0 / a[HEAD_DIM:HEAD_DIM + 1]))
    o_ref[...] = jnp.concatenate(outs, axis=0).T.astype(o_ref.dtype)


def _attn_prompt(z, c, tq, kt_all, vt_all, layer):
    b, l, _ = z.shape
    assert b * N_HEADS <= LANES
    hp = ATT_W // LANES
    tk = tq // 2
    return pl.pallas_call(
        functools.partial(_attn_kernel, tq=tq, tk=tk),
        grid=(b, hp, l // tq),
        in_specs=[
            pl.BlockSpec((None, tq, LANES), lambda bi, h, i: (bi, i, h)),
            pl.BlockSpec((None, l, LANES), lambda bi, h, i: (bi, 0, hp + h)),
            pl.BlockSpec((None, l, LANES), lambda bi, h, i: (bi, 0, 2 * hp + h)),
            pl.BlockSpec((None, l, LANES), lambda bi, h, i: (0, 0, 0)),
            pl.BlockSpec(memory_space=pl.ANY),
            pl.BlockSpec(memory_space=pl.ANY),
        ],
        out_specs=[
            pl.BlockSpec((None, tq, LANES), lambda bi, h, i: (bi, i, h)),
            pl.BlockSpec((None, None, LANES, tq), lambda bi, h, i: (layer, bi, h, i)),
            pl.BlockSpec((None, None, LANES, tq), lambda bi, h, i: (layer, bi, h, i)),
        ],
        out_shape=[
            jax.ShapeDtypeStruct((b, l, ATT_W), BF16),
            jax.ShapeDtypeStruct(kt_all.shape, F32),
            jax.ShapeDtypeStruct(vt_all.shape, F32),
        ],
        input_output_aliases={4: 1, 5: 2},
        scratch_shapes=[
            pltpu.VMEM((2, l, LANES), BF16),
            pltpu.VMEM((2, l // tk, V_ROWS, tk), BF16),
            pltpu.VMEM((2, 1, tq), F32),
            pltpu.VMEM((2, V_ROWS, tq), F32),
            pltpu.VMEM((2, tk, tq), F32),
            pltpu.VMEM((2, tk, tq), F32),
        ],
        compiler_params=_params("arbitrary", "arbitrary", "arbitrary"),
        name="attn_prompt",
    )(z, z, z, c, kt_all, vt_all)


def _attn_cache_kernel(q_ref, kn_ref, vn_ref, kc_ref, vc_ref, nckc_ref, nckn_ref, o_ref):
    q = q_ref[...]
    lq = q.shape[0]
    masks = _head_masks()
    kc = kc_ref[...].astype(BF16)
    vc = vc_ref[...].astype(BF16)
    kn = kn_ref[...].astype(BF16)
    vn = vn_ref[...].astype(BF16)
    nt_dims = (((1,), (1,)), ((), ()))
    r = lax.broadcasted_iota(jnp.int32, (lq, lq), 0)
    c = lax.broadcasted_iota(jnp.int32, (lq, lq), 1)
    outs = []
    for h in range(2):
        qh = jnp.where(masks[h], q, 0.0).astype(BF16)
        sc = jnp.dot(qh, kc, preferred_element_type=F32) + nckc_ref[pl.ds(h, 1), :]
        sn = lax.dot_general(qh, kn, nt_dims, preferred_element_type=F32) + nckn_ref[pl.ds(h, 1), :]
        sn = jnp.where(c <= r, sn, -jnp.inf)
        m = jnp.maximum(jnp.max(sc, axis=-1, keepdims=True), jnp.max(sn, axis=-1, keepdims=True))
        pc = jnp.exp(sc - m)
        pn = jnp.exp(sn - m)
        den = jnp.sum(pc, axis=-1, keepdims=True) + jnp.sum(pn, axis=-1, keepdims=True)
        acc = (lax.dot_general(pc.astype(BF16), vc, nt_dims, preferred_element_type=F32)
               + jnp.dot(pn.astype(BF16), vn, preferred_element_type=F32))
        outs.append(acc / den)
    o_ref[...] = jnp.where(masks[0], outs[0], outs[1]).astype(o_ref.dtype)


def _attn_sample(z, cache_k, cache_v, layer, nck_c, nck_n):
    b, lq, _ = z.shape
    p = cache_k.shape[3]
    hp = ATT_W // LANES
    return pl.pallas_call(
        _attn_cache_kernel,
        grid=(b, hp),
        in_specs=[
            pl.BlockSpec((None, lq, LANES), lambda bi, h: (bi, 0, h)),
            pl.BlockSpec((None, lq, LANES), lambda bi, h: (bi, 0, hp + h)),
            pl.BlockSpec((None, lq, LANES), lambda bi, h: (bi, 0, 2 * hp + h)),
            pl.BlockSpec((None, None, LANES, p), lambda bi, h: (layer, bi, h, 0)),
            pl.BlockSpec((None, None, LANES, p), lambda bi, h: (layer, bi, h, 0)),
            pl.BlockSpec((None, None, 2, p), lambda bi, h: (bi, h, 0, 0)),
            pl.BlockSpec((None, None, 2, lq), lambda bi, h: (bi, h, 0, 0)),
        ],
        out_specs=pl.BlockSpec((None, lq, LANES), lambda bi, h: (bi, 0, h)),
        out_shape=jax.ShapeDtypeStruct((b, lq, ATT_W), BF16),
        compiler_params=_params("arbitrary", "arbitrary"),
        name="attn_sample",
    )(z, z, z, cache_k, cache_v, nck_c, nck_n)


S5_TILE_GROUPS = LANES // S5_GROUP
S5_TILE_STATE = 2 * S5_TILE_GROUPS * S5_STATE


def _s5_kernel(u_ref, bw_ref, bd_ref, bv_ref, are_ref, aim_ref, d_ref, h0_ref,
               y_ref, send_ref, s_ref, e_ref, sp_ref, *, nb, cb):
    t = pl.program_id(1)
    tc = S5_CHUNK
    rows = nb * cb

    @pl.when(t == 0)
    def _():
        s_ref[...] = h0_ref[...]

    def tokens(j):
        if cb == 1:
            return u_ref[:, j, :]
        return u_ref[:, pl.ds(j, cb, stride=tc), :].reshape(rows, LANES)

    us = [tokens(j) for j in range(tc)]
    ucat = jnp.concatenate([u.astype(BF16) for u in us], axis=1)
    e = jnp.dot(ucat, bw_ref[...], preferred_element_type=F32)
    e_ref[...] = e.reshape(nb, cb, S5_TILE_STATE)

    a_re = are_ref[...]
    a_im = aim_ref[...]

    def advance(c, carry):
        for b in range(nb):
            s = s_ref[b]
            sp_ref[b, pl.ds(c, 1), :] = s
            s_ref[b] = a_re * s + a_im * pltpu.roll(s, S5_TILE_STATE // 2, 1) + e_ref[b, pl.ds(c, 1), :]
        return carry

    lax.fori_loop(0, cb, advance, 0)

    sp = sp_ref[...].reshape(rows, S5_TILE_STATE).astype(BF16)
    d = d_ref[...]
    y_state = jnp.dot(sp, bv_ref[...], preferred_element_type=F32)
    pair = 2 * LANES
    for k in range(tc // 2):
        kk = pair * (k + 1)
        yk = jnp.dot(ucat[:, :kk], bd_ref[:kk, k * pair:(k + 1) * pair], preferred_element_type=F32)
        yk = yk + y_state[:, k * pair:(k + 1) * pair]
        for jj in range(2):
            jo = 2 * k + jj
            y = yk[:, jj * LANES:(jj + 1) * LANES] + us[jo] * d
            if cb == 1:
                y_ref[:, jo, :] = y
            else:
                y_ref[:, pl.ds(jo, cb, stride=tc), :] = y.reshape(nb, cb, LANES)

    @pl.when(t == pl.num_programs(1) - 1)
    def _():
        send_ref[...] = s_ref[...]


def _s5(z3, u_col, mats, h0, cb):
    bw, bd, bv, a_re, a_im, d = mats
    nb, l, _ = z3.shape
    tiles = bw.shape[0]
    tc = S5_CHUNK
    blk = tc * cb
    col0 = u_col // LANES
    return pl.pallas_call(
        functools.partial(_s5_kernel, nb=nb, cb=cb),
        grid=(tiles, l // blk),
        in_specs=[
            pl.BlockSpec((nb, blk, LANES), lambda g, t: (0, t, col0 + g)),
            pl.BlockSpec((None, tc * LANES, S5_TILE_STATE), lambda g, t: (g, 0, 0)),
            pl.BlockSpec((None, tc * LANES, tc * LANES), lambda g, t: (g, 0, 0)),
            pl.BlockSpec((None, S5_TILE_STATE, tc * LANES), lambda g, t: (g, 0, 0)),
            pl.BlockSpec((None, 1, S5_TILE_STATE), lambda g, t: (g, 0, 0)),
            pl.BlockSpec((None, 1, S5_TILE_STATE), lambda g, t: (g, 0, 0)),
            pl.BlockSpec((None, 1, LANES), lambda g, t: (g, 0, 0)),
            pl.BlockSpec((nb, None, 1, S5_TILE_STATE), lambda g, t: (0, g, 0, 0)),
        ],
        out_specs=[
            pl.BlockSpec((nb, blk, LANES), lambda g, t: (0, t, g)),
            pl.BlockSpec((nb, None, 1, S5_TILE_STATE), lambda g, t: (0, g, 0, 0)),
        ],
        out_shape=[jax.ShapeDtypeStruct((nb, l, tiles * LANES), F32), jax.ShapeDtypeStruct(h0.shape, F32)],
        scratch_shapes=[
            pltpu.VMEM((nb, 1, S5_TILE_STATE), F32),
            pltpu.VMEM((nb, cb, S5_TILE_STATE), F32),
            pltpu.VMEM((nb, cb, S5_TILE_STATE), F32),
        ],
        compiler_params=_params("arbitrary", "arbitrary"),
        name="s5",
    )(z3, bw, bd, bv, a_re, a_im, d, h0)


def _s5_matrices(lam_re, lam_im, log_dt, b_re, b_im, c_re, c_im, d_skip):
    g, p = lam_re.shape
    tc = S5_CHUNK
    dt = jnp.exp(log_dt)[:, None]
    k = jnp.arange(tc + 1, dtype=F32)[:, None, None]
    mag = jnp.exp(lam_re * dt * k)
    pw_re = mag * jnp.cos(lam_im * dt * k)
    pw_im = mag * jnp.sin(lam_im * dt * k)
    den = lam_re * lam_re + lam_im * lam_im
    nr = pw_re[1] - 1.0
    f_re = (nr * lam_re + pw_im[1] * lam_im) / den
    f_im = (pw_im[1] * lam_re - nr * lam_im) / den
    bb_re = f_re[..., None] * b_re - f_im[..., None] * b_im
    bb_im = f_re[..., None] * b_im + f_im[..., None] * b_re
    ein = functools.partial(jnp.einsum, precision=HIGHEST)
    cp_re = c_re[None] * pw_re[:tc, :, None, :] - c_im[None] * pw_im[:tc, :, None, :]
    cp_im = c_re[None] * pw_im[:tc, :, None, :] + c_im[None] * pw_re[:tc, :, None, :]
    taps = ein('tghp,gpk->tghk', cp_re, bb_re) - ein('tghp,gpk->tghk', cp_im, bb_im)
    k_rev = (tc - 1) - jnp.arange(tc, dtype=F32)[:, None, None]
    mag_rev = jnp.exp(lam_re * dt * k_rev)
    rp_re = (mag_rev * jnp.cos(lam_im * dt * k_rev))[:, :, :, None]
    rp_im = (mag_rev * jnp.sin(lam_im * dt * k_rev))[:, :, :, None]
    w_re = rp_re * bb_re[None] - rp_im * bb_im[None]
    w_im = rp_re * bb_im[None] + rp_im * bb_re[None]
    z_re = c_re[None] * pw_re[1:, :, None, :] - c_im[None] * pw_im[1:, :, None, :]
    z_im = c_re[None] * pw_im[1:, :, None, :] + c_im[None] * pw_re[1:, :, None, :]
    gl = S5_TILE_GROUPS
    tiles = g // gl
    h = S5_GROUP
    tok_lane = jnp.arange(tc * LANES)
    st_lane = jnp.arange(S5_TILE_STATE)
    tok_group = (tok_lane // h) % gl
    st_group = (st_lane // p) % gl
    cmp_lane = jnp.arange(2 * p)
    rep_state = ((st_lane[None, :] // (gl * p) == cmp_lane[:, None] // p)
                 & (st_lane[None, :] % p == cmp_lane[:, None] % p)).astype(BF16)

    def expand(values, rep, row_group, col_group, rep_rows=False):
        spec = 'rq,tqc->trc' if rep_rows else 'trq,qc->trc'
        args = (rep.T, values.astype(BF16)) if rep_rows else (values.astype(BF16), rep)
        wide = jnp.einsum(spec, *args, preferred_element_type=F32)
        return jnp.where(row_group[:, None] == col_group[None, :], wide, 0.0).astype(BF16)

    tp = taps.reshape(tc, tiles, gl, h, h).transpose(1, 0, 2, 4, 3)
    eye = jnp.eye(gl, dtype=F32)
    bd = (tp[:, :, :, :, None, :] * eye[None, None, :, None, :, None]).reshape(tiles, tc, LANES, LANES).astype(BF16)
    lag = jnp.arange(tc)[None, :] - jnp.arange(tc)[:, None]
    bd = jnp.where((lag >= 0)[None, :, :, None, None], bd[:, jnp.clip(lag, 0, tc - 1)], 0)
    bd = bd.transpose(0, 1, 3, 2, 4).reshape(tiles, tc * LANES, tc * LANES)
    ws = jnp.stack([w_re, w_im]).reshape(2, tc, tiles, gl, p, h).transpose(2, 1, 3, 5, 0, 4)
    bw = expand(ws.reshape(tiles, tc * LANES, 2 * p), rep_state, tok_group, st_group)
    zs = jnp.stack([z_re, -z_im]).reshape(2, tc, tiles, gl, h, p).transpose(2, 0, 5, 1, 3, 4)
    bv = expand(zs.reshape(tiles, 2 * p, tc * LANES), rep_state, st_group, tok_group, rep_rows=True)
    ar = pw_re[tc].reshape(tiles, gl * p)
    ai = pw_im[tc].reshape(tiles, gl * p)
    a_re = jnp.concatenate([ar, ar], axis=-1)[:, None, :]
    a_im = jnp.concatenate([-ai, ai], axis=-1)[:, None, :]
    d = d_skip.reshape(tiles, 1, LANES)
    return bw.astype(BF16), bd.astype(BF16), bv.astype(BF16), a_re, a_im, d


def _s5_pack_state(re, im):
    b, g, p = re.shape
    tiles = g // S5_TILE_GROUPS
    return jnp.concatenate([re.reshape(b, tiles, 1, -1), im.reshape(b, tiles, 1, -1)], axis=-1)


def _s5_unpack_state(s, g):
    b = s.shape[0]
    half = S5_TILE_STATE // 2
    return s[..., :half].reshape(b, g, S5_STATE), s[..., half:].reshape(b, g, S5_STATE)


def _mix_kernel(o_ref, y_ref, ga_ref, gb_ref, wa_ref, wv_ref, wg_ref, out_ref, g5_ref):
    @pl.when(pl.program_id(1) == 0)
    def _():
        g5_ref[...] = _gelu_tanh(y_ref[...].astype(F32)).astype(BF16)

    ya = jnp.dot(o_ref[...], wa_ref[...], preferred_element_type=F32)
    g5 = g5_ref[...]
    yb = (jnp.dot(g5, wv_ref[...], preferred_element_type=F32)
          * _sigmoid(jnp.dot(g5, wg_ref[...], preferred_element_type=F32)))
    out = _sigmoid(ga_ref[...]) * ya + _sigmoid(gb_ref[...]) * yb
    out_ref[...] = out.astype(out_ref.dtype)


def _mix(o, y, z, gate_col, wa, wv, wg, tm, tn):
    t, kdim = o.shape
    n = wa.shape[1]
    ga0 = gate_col // tn
    gb0 = (gate_col + n) // tn
    return pl.pallas_call(
        _mix_kernel,
        grid=(t // tm, n // tn),
        in_specs=[
            pl.BlockSpec((tm, kdim), lambda i, j: (i, 0)),
            pl.BlockSpec((tm, kdim), lambda i, j: (i, 0)),
            pl.BlockSpec((tm, tn), lambda i, j: (i, ga0 + j)),
            pl.BlockSpec((tm, tn), lambda i, j: (i, gb0 + j)),
            pl.BlockSpec((kdim, tn), lambda i, j: (0, j)),
            pl.BlockSpec((kdim, tn), lambda i, j: (0, j)),
            pl.BlockSpec((kdim, tn), lambda i, j: (0, j)),
        ],
        out_specs=pl.BlockSpec((tm, tn), lambda i, j: (i, j)),
        out_shape=jax.ShapeDtypeStruct((t, n), BF16),
        scratch_shapes=[pltpu.VMEM((tm, kdim), BF16)],
        compiler_params=_params("arbitrary", "arbitrary"),
        name="mix",
    )(o, y, z, z, wa, wv, wg)


def _mm_norm_kernel(a_ref, w_ref, x_ref, g_ref, out_ref, *, k_steps):
    def product():
        return jnp.dot(a_ref[...], w_ref[...], preferred_element_type=F32)

    def finish(m):
        out_ref[...] = x_ref[...] + (m * _rms_scale(m)) * g_ref[...]

    if k_steps == 1:
        finish(product())
        return
    k = pl.program_id(1)

    @pl.when(k == 0)
    def _():
        out_ref[...] = product()

    @pl.when((k > 0) & (k < k_steps - 1))
    def _():
        out_ref[...] += product()

    @pl.when(k == k_steps - 1)
    def _():
        finish(out_ref[...] + product())


def _mm_norm(a, w, x, g, tm, tk):
    t, kdim = a.shape
    n = w.shape[1]
    return pl.pallas_call(
        functools.partial(_mm_norm_kernel, k_steps=kdim // tk),
        grid=(t // tm, kdim // tk),
        in_specs=[
            pl.BlockSpec((tm, tk), lambda i, k: (i, k)),
            pl.BlockSpec((tk, n), lambda i, k: (k, 0)),
            pl.BlockSpec((tm, n), lambda i, k: (i, 0)),
            pl.BlockSpec((1, n), lambda i, k: (0, 0)),
        ],
        out_specs=pl.BlockSpec((tm, n), lambda i, k: (i, 0)),
        out_shape=jax.ShapeDtypeStruct((t, n), F32),
        compiler_params=_params("arbitrary", "arbitrary"),
        name="mm_norm",
    )(a, w, x, g)


def _ffn_up_kernel(x_ref, g_ref, wa_ref, wb_ref, cw_ref, buf_ref, act_ref, nbuf_ref, h_ref, tail_ref,
                   *, rows_outer, blocks_per_seq):
    if rows_outer:
        i, j = pl.program_id(0), pl.program_id(1)
        fresh_rows = j == 0
    else:
        j, i = pl.program_id(0), pl.program_id(1)
        fresh_rows = True

    def normalise():
        x = x_ref[...]
        h_ref[...] = ((x * _rms_scale(x)) * g_ref[...]).astype(BF16)

    if rows_outer:
        pl.when(fresh_rows)(normalise)
    else:
        normalise()

    if blocks_per_seq > 1:
        @pl.when((i == 0) & (j == 0))
        def _():
            tail_ref[...] = jnp.zeros_like(tail_ref)

    h = h_ref[...]
    a = jnp.dot(h, wa_ref[...], preferred_element_type=F32)
    b = jnp.dot(h, wb_ref[...], preferred_element_type=F32)
    tm = a.shape[0]
    if blocks_per_seq == 1:
        prev = buf_ref[...]
    else:
        prev = jnp.where(i % blocks_per_seq == 0, buf_ref[...], tail_ref[j])
    last = a[tm - SUBLANES:, :]
    tail_ref[j] = last
    nbuf_ref[...] = last
    row = lax.broadcasted_iota(jnp.int32, a.shape, 0)
    p1 = prev[SUBLANES - 1:SUBLANES, :]
    p2 = prev[SUBLANES - 2:SUBLANES - 1, :]
    a1 = jnp.where(row == 0, p1, pltpu.roll(a, 1, 0))
    a2 = jnp.where(row == 0, p2, jnp.where(row == 1, p1, pltpu.roll(a, 2, 0)))
    cw = cw_ref[...]
    conv = a2 * cw[0:1, :] + a1 * cw[1:2, :] + a * cw[2:3, :]
    act_ref[...] = (_gelu_tanh(conv) * b).astype(act_ref.dtype)


def _ffn_up(x, g, w_up, cw, buf, seq_len, tm, tn, rows_outer):
    t, d = x.shape
    f = w_up.shape[1] // 2
    nj = f // tn
    bps = seq_len // tm
    if rows_outer:
        grid = (t // tm, nj)
        ij = lambda a, b: (a, b)
    else:
        grid = (nj, t // tm)
        ij = lambda a, b: (b, a)

    def spec(shape, fn):
        return pl.BlockSpec(shape, lambda a, b: fn(*ij(a, b)))

    return pl.pallas_call(
        functools.partial(_ffn_up_kernel, rows_outer=rows_outer, blocks_per_seq=bps),
        grid=grid,
        in_specs=[
            spec((tm, d), lambda i, j: (i, 0)),
            spec((1, d), lambda i, j: (0, 0)),
            spec((d, tn), lambda i, j: (0, j)),
            spec((d, tn), lambda i, j: (0, nj + j)),
            spec((SUBLANES, tn), lambda i, j: (0, j)),
            spec((None, SUBLANES, tn), lambda i, j: (i // bps, 0, j)),
        ],
        out_specs=[
            spec((tm, tn), lambda i, j: (i, j)),
            spec((None, SUBLANES, tn), lambda i, j: (i, 0, j)),
        ],
        out_shape=[jax.ShapeDtypeStruct((t, f), BF16), jax.ShapeDtypeStruct((t // tm, SUBLANES, f), F32)],
        scratch_shapes=[pltpu.VMEM((tm, d), BF16), pltpu.VMEM((nj, SUBLANES, tn), F32)],
        compiler_params=_params("arbitrary", "arbitrary"),
        name="ffn_up",
    )(x, g, w_up, w_up, cw, buf)


def _pick(t, pref):
    return pref if t % pref == 0 else t


def _tile_plan(t, l, d, f):
    tm = _pick(t, 1024)
    long_seq = l % tm == 0
    return dict(
        tm=tm,
        tn=512,
        tn_in=1024,
        tq=1024,
        long_seq=long_seq,
        tm_up=tm if long_seq else l,
        s5_chunks=min(64, l // S5_CHUNK),
        tm_mm=_pick(t, 512),
        tk_out=d,
        tk_down=f // 4 if (f // 4) % LANES == 0 else 512,
    )


def _trunk(x, weights, caches):
    b, l, d = x.shape
    t = b * l
    depth = weights['w_main'].shape[0]
    n_main = weights['w_main'].shape[2]
    gate_col = n_main - 2 * d
    u_col = 3 * ATT_W
    g_s5 = weights['lam_re'].shape[1]
    f = weights['w_down'].shape[1]
    tp = _tile_plan(t, l, d, f)
    tm, tn, tm_up, long_seq = tp['tm'], tp['tn'], tp['tm_up'], tp['long_seq']
    hp = ATT_W // LANES
    x = x.reshape(t, d)
    st = {k: [] for k in ('k', 'v', 'logf', 're', 'im', 'conv')}
    if caches is None:
        kt_all = jnp.zeros((depth, b, ATT_W, l), F32)
        vt_all = jnp.zeros((depth, b, ATT_W, l), F32)
    for layer in range(depth):
        wl = {k: v[layer] for k, v in weights.items()}
        z, logf = _norm_in(x, wl['g_mix_pre'], wl['w_main'], wl['w_f'], wl['b_f'], tm, tp['tn_in'])
        lf = logf[:, :N_HEADS].reshape(b, l, N_HEADS)
        st['logf'].append(lf)
        z3 = z.reshape(b, l, n_main)
        if caches is None:
            c = _cumsum_rows(_lanes_by_sequence(lf, l), CUMSUM_ROWS)
            o, kt_all, vt_all = _attn_prompt(z3, c, tp['tq'], kt_all, vt_all, layer)
            h0 = jnp.zeros((b, g_s5 // S5_TILE_GROUPS, 1, S5_TILE_STATE), F32)
            buf = jnp.zeros((b, SUBLANES, f), F32)
        else:
            st['k'].append(z[:, ATT_W:2 * ATT_W].reshape(b, l, N_HEADS, HEAD_DIM))
            st['v'].append(z[:, 2 * ATT_W:3 * ATT_W].reshape(b, l, N_HEADS, HEAD_DIM))
            past = caches['logf'].shape[2]
            lf_all = jnp.concatenate([caches['logf'][layer], lf], axis=1)
            padded = -(-(past + l) // CUMSUM_ROWS) * CUMSUM_ROWS
            c_all = _cumsum_rows(_lanes_by_sequence(lf_all, padded), CUMSUM_ROWS)
            nck_all = -c_all[0, :, :b * N_HEADS].T.reshape(b, N_HEADS, padded)
            nck_c = nck_all[:, :, :past].reshape(b, hp, 2, past)
            nck_n = nck_all[:, :, past:past + l].reshape(b, hp, 2, l)
            o = _attn_sample(z3, caches['k'], caches['v'], layer, nck_c, nck_n)
            h0 = _s5_pack_state(caches['ssm_re'][layer], caches['ssm_im'][layer])
            buf = jnp.pad(caches['conv'][layer], ((0, 0), (SUBLANES - (CONV_W - 1), 0), (0, 0)))
        o = o.reshape(t, ATT_W)
        y, s_end = _s5(z3, u_col, weights['s5_mats'][layer], h0, tp['s5_chunks'])
        y = y.reshape(t, g_s5 * S5_GROUP)
        s_re, s_im = _s5_unpack_state(s_end, g_s5)
        st['re'].append(s_re)
        st['im'].append(s_im)
        mix_in = _mix(o, y, z, gate_col, wl['w_att_proj'], wl['w_glu_v'], wl['w_glu_g'], tm, tn)
        x = _mm_norm(mix_in, wl['w_out'], x, wl['g_mix_post'], tp['tm_mm'], tp['tk_out'])
        act, nbuf = _ffn_up(x, wl['g_ffn_pre'], wl['w_up'], wl['conv_w'], buf, l, tm_up, tn, long_seq)
        nbuf = nbuf.reshape(b, l // tm_up, SUBLANES, f)[:, -1]
        st['conv'].append(nbuf[:, SUBLANES - (CONV_W - 1):])
        x = _mm_norm(act, wl['w_down'], x, wl['g_ffn_post'], tp['tm_mm'], tp['tk_down'])
    stacked = [jnp.stack(st[k]) for k in ('logf', 're', 'im', 'conv')]
    if caches is None:
        kv = [a.reshape(depth, b, N_HEADS, HEAD_DIM, l).transpose(0, 1, 4, 2, 3) for a in (kt_all, vt_all)]
    else:
        kv = [jnp.stack(st['k']), jnp.stack(st['v'])]
    return x.reshape(b, l, d), kv + stacked


def kernel(x_prompt, x_sample, cache_k, cache_v, cache_logf, state_ssm_re, state_ssm_im, state_conv,
           g_mix_pre, w_in, b_f, lam_re, lam_im, log_dt, b_re, b_im, c_re, c_im, d_skip,
           w_att_proj, w_glu_v, w_glu_g, w_out, g_mix_post, g_ffn_pre, w_up, conv_w, w_down, g_ffn_post):
    depth, d, _ = w_in.shape
    s5_w = d_skip.shape[1]
    f_lo, f_hi = 3 * ATT_W, 3 * ATT_W + N_HEADS
    scale = HEAD_DIM ** -0.5
    w_main = jnp.concatenate([w_in[:, :, :ATT_W] * scale, w_in[:, :, ATT_W:f_lo], w_in[:, :, f_hi:]], axis=2)
    weights = {
        'w_main': w_main.astype(BF16),
        'w_f': jnp.pad(w_in[:, :, f_lo:f_hi], ((0, 0), (0, 0), (0, LANES - N_HEADS))).astype(BF16),
        'b_f': jnp.pad(b_f, ((0, 0), (0, LANES - N_HEADS))).reshape(depth, 1, LANES),
        'g_mix_pre': g_mix_pre.reshape(depth, 1, d),
        'g_mix_post': g_mix_post.reshape(depth, 1, d),
        'g_ffn_pre': g_ffn_pre.reshape(depth, 1, d),
        'g_ffn_post': g_ffn_post.reshape(depth, 1, d),
        'lam_re': lam_re,
        's5_mats': [_s5_matrices(lam_re[n], lam_im[n], log_dt[n], b_re[n], b_im[n], c_re[n], c_im[n], d_skip[n])
                    for n in range(depth)],
        'w_att_proj': w_att_proj.astype(BF16),
        'w_glu_v': w_glu_v.astype(BF16),
        'w_glu_g': w_glu_g.astype(BF16),
        'w_out': w_out.astype(BF16),
        'w_up': w_up.astype(BF16),
        'conv_w': jnp.pad(conv_w, ((0, 0), (0, SUBLANES - CONV_W), (0, 0))),
        'w_down': w_down.astype(BF16),
    }
    assert s5_w == lam_re.shape[1] * S5_GROUP
    sb, sp = cache_k.shape[1], cache_k.shape[2]
    caches = {
        'k': cache_k.transpose(0, 1, 3, 4, 2).reshape(depth, sb, ATT_W, sp),
        'v': cache_v.transpose(0, 1, 3, 4, 2).reshape(depth, sb, ATT_W, sp),
        'logf': cache_logf, 'ssm_re': state_ssm_re, 'ssm_im': state_ssm_im, 'conv': state_conv,
    }
    y_prompt, p_st = _trunk(x_prompt, weights, None)
    y_sample, s_st = _trunk(x_sample, weights, caches)
    return (y_prompt, y_sample, *p_st, *s_st)
```

```python
import functools

import jax
import jax.numpy as jnp
from jax import lax
from jax.experimental import pallas as pl
from jax.experimental.pallas import tpu as pltpu

F32 = jnp.float32
BF16 = jnp.bfloat16
HIGHEST = lax.Precision.HIGHEST

RMS_EPS = 1e-6
N_HEADS = 16
HEAD_DIM = 64
ATT_W = N_HEADS * HEAD_DIM
S5_GROUP = 16
S5_STATE = 64
CONV_W = 3
LANES = 128
SUBLANES = 8
S5_CHUNK = 16
VMEM_LIMIT_BYTES = 56 * 1024 * 1024


def _params(*sem):
    return pltpu.CompilerParams(dimension_semantics=sem, vmem_limit_bytes=VMEM_LIMIT_BYTES)


def _gelu_tanh(x):
    return 0.5 * x * (1.0 + jnp.tanh(0.7978845608028654 * (x + 0.044715 * (x * x * x))))


def _sigmoid(x):
    return 0.5 * jnp.tanh(0.5 * x) + 0.5


def _log_sigmoid(x):
    return jnp.minimum(x, 0.0) - jnp.log1p(jnp.exp(-jnp.abs(x)))


def _rms_scale(x):
    return lax.rsqrt(jnp.mean(x * x, axis=-1, keepdims=True) + RMS_EPS)


def _norm_in_kernel(x_ref, g_ref, w_ref, wf_ref, bf_ref, z_ref, lf_ref, h_ref):
    @pl.when(pl.program_id(1) == 0)
    def _():
        x = x_ref[...]
        h = ((x * _rms_scale(x)) * g_ref[...]).astype(BF16)
        h_ref[...] = h
        fl = jnp.dot(h, wf_ref[...], preferred_element_type=F32) + bf_ref[...]
        lf_ref[...] = _log_sigmoid(fl)

    z_ref[...] = jnp.dot(h_ref[...], w_ref[...], preferred_element_type=F32)


def _norm_in(x, g, w, wf, bf, tm, tn):
    t, d = x.shape
    n = w.shape[1]
    return pl.pallas_call(
        _norm_in_kernel,
        grid=(t // tm, n // tn),
        in_specs=[
            pl.BlockSpec((tm, d), lambda i, j: (i, 0)),
            pl.BlockSpec((1, d), lambda i, j: (0, 0)),
            pl.BlockSpec((d, tn), lambda i, j: (0, j)),
            pl.BlockSpec((d, LANES), lambda i, j: (0, 0)),
            pl.BlockSpec((1, LANES), lambda i, j: (0, 0)),
        ],
        out_specs=[
            pl.BlockSpec((tm, tn), lambda i, j: (i, j)),
            pl.BlockSpec((tm, LANES), lambda i, j: (i, 0)),
        ],
        out_shape=[jax.ShapeDtypeStruct((t, n), F32), jax.ShapeDtypeStruct((t, LANES), F32)],
        scratch_shapes=[pltpu.VMEM((tm, d), BF16)],
        compiler_params=_params("arbitrary", "arbitrary"),
        name="norm_in",
    )(x, g, w, wf, bf)


CUMSUM_ROWS = 512


def _cumsum_rows_kernel(x_ref, o_ref, carry_ref):
    @pl.when(pl.program_id(1) == 0)
    def _():
        carry_ref[...] = jnp.zeros_like(carry_ref)

    n = x_ref.shape[0]
    row = lax.broadcasted_iota(jnp.int32, (n, n), 0)
    col = lax.broadcasted_iota(jnp.int32, (n, n), 1)
    lower = (col <= row).astype(F32)
    c = jnp.dot(lower, x_ref[...], precision=HIGHEST, preferred_element_type=F32) + carry_ref[...]
    o_ref[...] = c
    carry_ref[...] = c[n - 1:n, :]


def _lanes_by_sequence(lf, rows):
    b, l, h = lf.shape
    lanes = -(-(b * h) // LANES) * LANES
    x = lf.transpose(1, 0, 2).reshape(l, b * h)
    return jnp.pad(x, ((0, rows - l), (0, lanes - b * h)))[None]


def _cumsum_rows(x, blk):
    b, l, w = x.shape
    return pl.pallas_call(
        _cumsum_rows_kernel,
        grid=(b, l // blk),
        in_specs=[pl.BlockSpec((None, blk, w), lambda i, j: (i, j, 0))],
        out_specs=pl.BlockSpec((None, blk, w), lambda i, j: (i, j, 0)),
        out_shape=jax.ShapeDtypeStruct((b, l, w), F32),
        scratch_shapes=[pltpu.VMEM((1, w), F32)],
        compiler_params=_params("arbitrary", "arbitrary"),
        name="cumsum_rows",
    )(x)


def _head_masks():
    lane = lax.broadcasted_iota(jnp.int32, (1, LANES), 1)
    return lane < HEAD_DIM, lane >= HEAD_DIM


BIAS_PARTS = 3
V_ROWS = HEAD_DIM + 16

def _attn_kernel(q_ref, k_ref, v_ref, c_ref, kt_all_ref, vt_all_ref, o_ref, kt_ref, vto_ref,
                 kx_ref, vt_ref, m_ref, acc_ref, sa_ref, sb_ref, *, tq, tk):
    del kt_all_ref, vt_all_ref
    seq = pl.program_id(0)
    hp = pl.program_id(1)
    i = pl.program_id(2)
    n_blk = k_ref.shape[0] // tk
    lane = lax.broadcasted_iota(jnp.int32, (1, LANES), 1)
    own = [(lane >= h * HEAD_DIM) & (lane < (h + 1) * HEAD_DIM) for h in range(2)]
    spare = [(1 - h) * HEAD_DIM for h in range(2)]

    @pl.when(i == 0)
    def _():
        er = lax.broadcasted_iota(jnp.int32, (LANES, LANES), 0)
        ec = lax.broadcasted_iota(jnp.int32, (LANES, LANES), 1)
        row = lax.broadcasted_iota(jnp.int32, (LANES, tk), 0)

        def prep(j, carry):
            r0 = pl.multiple_of(j * tk, tk)
            kblk = k_ref[pl.ds(r0, tk), :]
            nc = -c_ref[pl.ds(r0, tk), :]
            parts = []
            rem = nc
            for _ in range(BIAS_PARTS):
                piece = rem.astype(BF16)
                parts.append(piece)
                rem = rem - piece.astype(F32)
            vt = v_ref[pl.ds(r0, tk), :].T
            for h in range(2):
                bias = jnp.zeros((tk, LANES), F32)
                for n, piece in enumerate(parts):
                    sel = ((er == seq * N_HEADS + 2 * hp + h) & (ec == spare[h] + n)).astype(BF16)
                    bias = bias + jnp.dot(piece, sel, preferred_element_type=F32)
                kx_ref[h, pl.ds(r0, tk), :] = (jnp.where(own[h], kblk, 0.0) + bias).astype(BF16)
                vth = vt if h == 0 else pltpu.roll(vt, HEAD_DIM, 0)
                vt_ref[h, j] = jnp.where(row < HEAD_DIM, vth, 1.0)[:V_ROWS].astype(BF16)
            return carry

        lax.fori_loop(0, n_blk, prep, 0)

    own_rows = pl.ds(pl.multiple_of(i * tq, tq), tq)
    kt_ref[...] = k_ref[own_rows, :].T
    vto_ref[...] = v_ref[own_rows, :].T

    q = q_ref[...]
    qx = []
    for h in range(2):
        ones = (lane >= spare[h]) & (lane < spare[h] + BIAS_PARTS)
        qx.append((jnp.where(own[h], q, 0.0) + jnp.where(ones, 1.0, 0.0)).astype(BF16))
    m_ref[...] = jnp.full_like(m_ref, -jnp.inf)
    acc_ref[...] = jnp.zeros_like(acc_ref)
    nt_dims = (((1,), (1,)), ((), ()))

    def scores(kb, s_ref, q_from=0):
        start = pl.multiple_of(kb * tk, tk)
        for h in range(2):
            s_ref[h, :, q_from:] = lax.dot_general(kx_ref[h, pl.ds(start, tk), :], qx[h][q_from:], nt_dims,
                                                   preferred_element_type=F32)

    def absorb(kb, s_ref, first_key=None, q_from=0):
        nq = tq - q_from
        sts = [s_ref[h, :, q_from:] for h in range(2)]
        if first_key is not None:
            key = lax.broadcasted_iota(jnp.int32, (tk, nq), 0) + first_key
            qry = lax.broadcasted_iota(jnp.int32, (tk, nq), 1) + q_from
            sts = [jnp.where(key <= qry, st, -jnp.inf) for st in sts]
        m_old = [m_ref[h, :, q_from:] for h in range(2)]
        m_new = [jnp.maximum(m_old[h], jnp.max(sts[h], axis=0, keepdims=True)) for h in range(2)]
        pts = [jnp.exp(sts[h] - m_new[h]).astype(BF16) for h in range(2)]
        pvs = [jnp.dot(vt_ref[h, kb], pts[h], preferred_element_type=F32) for h in range(2)]
        for h in range(2):
            acc_ref[h, :, q_from:] = jnp.exp(m_old[h] - m_new[h]) * acc_ref[h, :, q_from:] + pvs[h]
            m_ref[h, :, q_from:] = m_new[h]

    scores(0, sa_ref)

    def pair(p, carry):
        kb = 2 * p
        scores(kb + 1, sb_ref)
        absorb(kb, sa_ref)
        scores(kb + 2, sa_ref)
        absorb(kb + 1, sb_ref)
        return carry

    lax.fori_loop(0, i, pair, 0)
    scores(2 * i + 1, sb_ref, q_from=tk)
    absorb(2 * i, sa_ref, first_key=0)
    absorb(2 * i + 1, sb_ref, first_key=tk, q_from=tk)

    outs = []
    for h in range(2):
        a = acc_ref[h]
        outs.append(a[:HEAD_DIM] * (1.0 / a[HEAD_DIM:HEAD_DIM + 1]))
    o_ref[...] = jnp.concatenate(outs, axis=0).T.astype(o_ref.dtype)


def _attn_prompt(z, c, tq, kt_all, vt_all, layer):
    b, l, _ = z.shape
    assert b * N_HEADS <= LANES
    hp = ATT_W // LANES
    tk = tq // 2
    return pl.pallas_call(
        functools.partial(_attn_kernel, tq=tq, tk=tk),
        grid=(b, hp, l // tq),
        in_specs=[
            pl.BlockSpec((None, tq, LANES), lambda bi, h, i: (bi, i, h)),
            pl.BlockSpec((None, l, LANES), lambda bi, h, i: (bi, 0, hp + h)),
            pl.BlockSpec((None, l, LANES), lambda bi, h, i: (bi, 0, 2 * hp + h)),
            pl.BlockSpec((None, l, LANES), lambda bi, h, i: (0, 0, 0)),
            pl.BlockSpec(memory_space=pl.ANY),
            pl.BlockSpec(memory_space=pl.ANY),
        ],
        out_specs=[
            pl.BlockSpec((None, tq, LANES), lambda bi, h, i: (bi, i, h)),
            pl.BlockSpec((None, None, LANES, tq), lambda bi, h, i: (layer, bi, h, i)),
            pl.BlockSpec((None, None, LANES, tq), lambda bi, h, i: (layer, bi, h, i)),
        ],
        out_shape=[
            jax.ShapeDtypeStruct((b, l, ATT_W), BF16),
            jax.ShapeDtypeStruct(kt_all.shape, F32),
            jax.ShapeDtypeStruct(vt_all.shape, F32),
        ],
        input_output_aliases={4: 1, 5: 2},
        scratch_shapes=[
            pltpu.VMEM((2, l, LANES), BF16),
            pltpu.VMEM((2, l // tk, V_ROWS, tk), BF16),
            pltpu.VMEM((2, 1, tq), F32),
            pltpu.VMEM((2, V_ROWS, tq), F32),
            pltpu.VMEM((2, tk, tq), F32),
            pltpu.VMEM((2, tk, tq), F32),
        ],
        compiler_params=_params("arbitrary", "arbitrary", "arbitrary"),
        name="attn_prompt",
    )(z, z, z, c, kt_all, vt_all)


def _attn_cache_kernel(q_ref, kn_ref, vn_ref, kc_ref, vc_ref, nckc_ref, nckn_ref, o_ref):
    q = q_ref[...]
    lq = q.shape[0]
    masks = _head_masks()
    kc = kc_ref[...].astype(BF16)
    vc = vc_ref[...].astype(BF16)
    kn = kn_ref[...].astype(BF16)
    vn = vn_ref[...].astype(BF16)
    nt_dims = (((1,), (1,)), ((), ()))
    r = lax.broadcasted_iota(jnp.int32, (lq, lq), 0)
    c = lax.broadcasted_iota(jnp.int32, (lq, lq), 1)
    outs = []
    for h in range(2):
        qh = jnp.where(masks[h], q, 0.0).astype(BF16)
        sc = jnp.dot(qh, kc, preferred_element_type=F32) + nckc_ref[pl.ds(h, 1), :]
        sn = lax.dot_general(qh, kn, nt_dims, preferred_element_type=F32) + nckn_ref[pl.ds(h, 1), :]
        sn = jnp.where(c <= r, sn, -jnp.inf)
        m = jnp.maximum(jnp.max(sc, axis=-1, keepdims=True), jnp.max(sn, axis=-1, keepdims=True))
        pc = jnp.exp(sc - m)
        pn = jnp.exp(sn - m)
        den = jnp.sum(pc, axis=-1, keepdims=True) + jnp.sum(pn, axis=-1, keepdims=True)
        acc = (lax.dot_general(pc.astype(BF16), vc, nt_dims, preferred_element_type=F32)
               + jnp.dot(pn.astype(BF16), vn, preferred_element_type=F32))
        outs.append(acc / den)
    o_ref[...] = jnp.where(masks[0], outs[0], outs[1]).astype(o_ref.dtype)


def _attn_sample(z, cache_k, cache_v, layer, nck_c, nck_n):
    b, lq, _ = z.shape
    p = cache_k.shape[3]
    hp = ATT_W // LANES
    return pl.pallas_call(
        _attn_cache_kernel,
        grid=(b, hp),
        in_specs=[
            pl.BlockSpec((None, lq, LANES), lambda bi, h: (bi, 0, h)),
            pl.BlockSpec((None, lq, LANES), lambda bi, h: (bi, 0, hp + h)),
            pl.BlockSpec((None, lq, LANES), lambda bi, h: (bi, 0, 2 * hp + h)),
            pl.BlockSpec((None, None, LANES, p), lambda bi, h: (layer, bi, h, 0)),
            pl.BlockSpec((None, None, LANES, p), lambda bi, h: (layer, bi, h, 0)),
            pl.BlockSpec((None, None, 2, p), lambda bi, h: (bi, h, 0, 0)),
            pl.BlockSpec((None, None, 2, lq), lambda bi, h: (bi, h, 0, 0)),
        ],
        out_specs=pl.BlockSpec((None, lq, LANES), lambda bi, h: (bi, 0, h)),
        out_shape=jax.ShapeDtypeStruct((b, lq, ATT_W), BF16),
        compiler_params=_params("arbitrary", "arbitrary"),
        name="attn_sample",
    )(z, z, z, cache_k, cache_v, nck_c, nck_n)


S5_TILE_GROUPS = LANES // S5_GROUP
S5_TILE_STATE = 2 * S5_TILE_GROUPS * S5_STATE


def _s5_kernel(u_ref, bw_ref, bd_ref, bv_ref, are_ref, aim_ref, d_ref, h0_ref,
               y_ref, send_ref, s_ref, e_ref, sp_ref, *, nb, cb):
    t = pl.program_id(1)
    tc = S5_CHUNK
    rows = nb * cb

    @pl.when(t == 0)
    def _():
        s_ref[...] = h0_ref[...]

    def tokens(j):
        if cb == 1:
            return u_ref[:, j, :]
        return u_ref[:, pl.ds(j, cb, stride=tc), :].reshape(rows, LANES)

    us = [tokens(j) for j in range(tc)]
    ucat = jnp.concatenate([u.astype(BF16) for u in us], axis=1)
    e = jnp.dot(ucat, bw_ref[...], preferred_element_type=F32)
    e_ref[...] = e.reshape(nb, cb, S5_TILE_STATE)

    a_re = are_ref[...]
    a_im = aim_ref[...]

    def advance(c, carry):
        for b in range(nb):
            s = s_ref[b]
            sp_ref[b, pl.ds(c, 1), :] = s
            s_ref[b] = a_re * s + a_im * pltpu.roll(s, S5_TILE_STATE // 2, 1) + e_ref[b, pl.ds(c, 1), :]
        return carry

    lax.fori_loop(0, cb, advance, 0)

    sp = sp_ref[...].reshape(rows, S5_TILE_STATE).astype(BF16)
    d = d_ref[...]
    y_state = jnp.dot(sp, bv_ref[...], preferred_element_type=F32)
    pair = 2 * LANES
    for k in range(tc // 2):
        kk = pair * (k + 1)
        yk = jnp.dot(ucat[:, :kk], bd_ref[:kk, k * pair:(k + 1) * pair], preferred_element_type=F32)
        yk = yk + y_state[:, k * pair:(k + 1) * pair]
        for jj in range(2):
            jo = 2 * k + jj
            y = yk[:, jj * LANES:(jj + 1) * LANES] + us[jo] * d
            if cb == 1:
                y_ref[:, jo, :] = y
            else:
                y_ref[:, pl.ds(jo, cb, stride=tc), :] = y.reshape(nb, cb, LANES)

    @pl.when(t == pl.num_programs(1) - 1)
    def _():
        send_ref[...] = s_ref[...]


def _s5(z3, u_col, mats, h0, cb):
    bw, bd, bv, a_re, a_im, d = mats
    nb, l, _ = z3.shape
    tiles = bw.shape[0]
    tc = S5_CHUNK
    blk = tc * cb
    col0 = u_col // LANES
    return pl.pallas_call(
        functools.partial(_s5_kernel, nb=nb, cb=cb),
        grid=(tiles, l // blk),
        in_specs=[
            pl.BlockSpec((nb, blk, LANES), lambda g, t: (0, t, col0 + g)),
            pl.BlockSpec((None, tc * LANES, S5_TILE_STATE), lambda g, t: (g, 0, 0)),
            pl.BlockSpec((None, tc * LANES, tc * LANES), lambda g, t: (g, 0, 0)),
            pl.BlockSpec((None, S5_TILE_STATE, tc * LANES), lambda g, t: (g, 0, 0)),
            pl.BlockSpec((None, 1, S5_TILE_STATE), lambda g, t: (g, 0, 0)),
            pl.BlockSpec((None, 1, S5_TILE_STATE), lambda g, t: (g, 0, 0)),
            pl.BlockSpec((None, 1, LANES), lambda g, t: (g, 0, 0)),
            pl.BlockSpec((nb, None, 1, S5_TILE_STATE), lambda g, t: (0, g, 0, 0)),
        ],
        out_specs=[
            pl.BlockSpec((nb, blk, LANES), lambda g, t: (0, t, g)),
            pl.BlockSpec((nb, None, 1, S5_TILE_STATE), lambda g, t: (0, g, 0, 0)),
        ],
        out_shape=[jax.ShapeDtypeStruct((nb, l, tiles * LANES), F32), jax.ShapeDtypeStruct(h0.shape, F32)],
        scratch_shapes=[
            pltpu.VMEM((nb, 1, S5_TILE_STATE), F32),
            pltpu.VMEM((nb, cb, S5_TILE_STATE), F32),
            pltpu.VMEM((nb, cb, S5_TILE_STATE), F32),
        ],
        compiler_params=_params("arbitrary", "arbitrary"),
        name="s5",
    )(z3, bw, bd, bv, a_re, a_im, d, h0)


def _s5_matrices(lam_re, lam_im, log_dt, b_re, b_im, c_re, c_im, d_skip):
    g, p = lam_re.shape
    tc = S5_CHUNK
    dt = jnp.exp(log_dt)[:, None]
    k = jnp.arange(tc + 1, dtype=F32)[:, None, None]
    mag = jnp.exp(lam_re * dt * k)
    pw_re = mag * jnp.cos(lam_im * dt * k)
    pw_im = mag * jnp.sin(lam_im * dt * k)
    den = lam_re * lam_re + lam_im * lam_im
    nr = pw_re[1] - 1.0
    f_re = (nr * lam_re + pw_im[1] * lam_im) / den
    f_im = (pw_im[1] * lam_re - nr * lam_im) / den
    bb_re = f_re[..., None] * b_re - f_im[..., None] * b_im
    bb_im = f_re[..., None] * b_im + f_im[..., None] * b_re
    ein = functools.partial(jnp.einsum, precision=HIGHEST)
    cp_re = c_re[None] * pw_re[:tc, :, None, :] - c_im[None] * pw_im[:tc, :, None, :]
    cp_im = c_re[None] * pw_im[:tc, :, None, :] + c_im[None] * pw_re[:tc, :, None, :]
    taps = ein('tghp,gpk->tghk', cp_re, bb_re) - ein('tghp,gpk->tghk', cp_im, bb_im)
    k_rev = (tc - 1) - jnp.arange(tc, dtype=F32)[:, None, None]
    mag_rev = jnp.exp(lam_re * dt * k_rev)
    rp_re = (mag_rev * jnp.cos(lam_im * dt * k_rev))[:, :, :, None]
    rp_im = (mag_rev * jnp.sin(lam_im * dt * k_rev))[:, :, :, None]
    w_re = rp_re * bb_re[None] - rp_im * bb_im[None]
    w_im = rp_re * bb_im[None] + rp_im * bb_re[None]
    z_re = c_re[None] * pw_re[1:, :, None, :] - c_im[None] * pw_im[1:, :, None, :]
    z_im = c_re[None] * pw_im[1:, :, None, :] + c_im[None] * pw_re[1:, :, None, :]
    gl = S5_TILE_GROUPS
    tiles = g // gl
    h = S5_GROUP
    tok_lane = jnp.arange(tc * LANES)
    st_lane = jnp.arange(S5_TILE_STATE)
    tok_group = (tok_lane // h) % gl
    st_group = (st_lane // p) % gl
    cmp_lane = jnp.arange(2 * p)
    rep_state = ((st_lane[None, :] // (gl * p) == cmp_lane[:, None] // p)
                 & (st_lane[None, :] % p == cmp_lane[:, None] % p)).astype(BF16)

    def expand(values, rep, row_group, col_group, rep_rows=False):
        spec = 'rq,tqc->trc' if rep_rows else 'trq,qc->trc'
        args = (rep.T, values.astype(BF16)) if rep_rows else (values.astype(BF16), rep)
        wide = jnp.einsum(spec, *args, preferred_element_type=F32)
        return jnp.where(row_group[:, None] == col_group[None, :], wide, 0.0).astype(BF16)

    tp = taps.reshape(tc, tiles, gl, h, h).transpose(1, 0, 2, 4, 3)
    eye = jnp.eye(gl, dtype=F32)
    bd = (tp[:, :, :, :, None, :] * eye[None, None, :, None, :, None]).reshape(tiles, tc, LANES, LANES).astype(BF16)
    lag = jnp.arange(tc)[None, :] - jnp.arange(tc)[:, None]
    bd = jnp.where((lag >= 0)[None, :, :, None, None], bd[:, jnp.clip(lag, 0, tc - 1)], 0)
    bd = bd.transpose(0, 1, 3, 2, 4).reshape(tiles, tc * LANES, tc * LANES)
    ws = jnp.stack([w_re, w_im]).reshape(2, tc, tiles, gl, p, h).transpose(2, 1, 3, 5, 0, 4)
    bw = expand(ws.reshape(tiles, tc * LANES, 2 * p), rep_state, tok_group, st_group)
    zs = jnp.stack([z_re, -z_im]).reshape(2, tc, tiles, gl, h, p).transpose(2, 0, 5, 1, 3, 4)
    bv = expand(zs.reshape(tiles, 2 * p, tc * LANES), rep_state, st_group, tok_group, rep_rows=True)
    ar = pw_re[tc].reshape(tiles, gl * p)
    ai = pw_im[tc].reshape(tiles, gl * p)
    a_re = jnp.concatenate([ar, ar], axis=-1)[:, None, :]
    a_im = jnp.concatenate([-ai, ai], axis=-1)[:, None, :]
    d = d_skip.reshape(tiles, 1, LANES)
    return bw.astype(BF16), bd.astype(BF16), bv.astype(BF16), a_re, a_im, d


def _s5_pack_state(re, im):
    b, g, p = re.shape
    tiles = g // S5_TILE_GROUPS
    return jnp.concatenate([re.reshape(b, tiles, 1, -1), im.reshape(b, tiles, 1, -1)], axis=-1)


def _s5_unpack_state(s, g):
    b = s.shape[0]
    half = S5_TILE_STATE // 2
    return s[..., :half].reshape(b, g, S5_STATE), s[..., half:].reshape(b, g, S5_STATE)


def _mix_kernel(o_ref, y_ref, ga_ref, gb_ref, wa_ref, wv_ref, wg_ref, out_ref, g5_ref):
    @pl.when(pl.program_id(1) == 0)
    def _():
        g5_ref[...] = _gelu_tanh(y_ref[...].astype(F32)).astype(BF16)

    ya = jnp.dot(o_ref[...], wa_ref[...], preferred_element_type=F32)
    g5 = g5_ref[...]
    yb = (jnp.dot(g5, wv_ref[...], preferred_element_type=F32)
          * _sigmoid(jnp.dot(g5, wg_ref[...], preferred_element_type=F32)))
    out = _sigmoid(ga_ref[...]) * ya + _sigmoid(gb_ref[...]) * yb
    out_ref[...] = out.astype(out_ref.dtype)


def _mix(o, y, z, gate_col, wa, wv, wg, tm, tn):
    t, kdim = o.shape
    n = wa.shape[1]
    ga0 = gate_col // tn
    gb0 = (gate_col + n) // tn
    return pl.pallas_call(
        _mix_kernel,
        grid=(t // tm, n // tn),
        in_specs=[
            pl.BlockSpec((tm, kdim), lambda i, j: (i, 0)),
            pl.BlockSpec((tm, kdim), lambda i, j: (i, 0)),
            pl.BlockSpec((tm, tn), lambda i, j: (i, ga0 + j)),
            pl.BlockSpec((tm, tn), lambda i, j: (i, gb0 + j)),
            pl.BlockSpec((kdim, tn), lambda i, j: (0, j)),
            pl.BlockSpec((kdim, tn), lambda i, j: (0, j)),
            pl.BlockSpec((kdim, tn), lambda i, j: (0, j)),
        ],
        out_specs=pl.BlockSpec((tm, tn), lambda i, j: (i, j)),
        out_shape=jax.ShapeDtypeStruct((t, n), BF16),
        scratch_shapes=[pltpu.VMEM((tm, kdim), BF16)],
        compiler_params=_params("arbitrary", "arbitrary"),
        name="mix",
    )(o, y, z, z, wa, wv, wg)


def _mm_norm_kernel(a_ref, w_ref, x_ref, g_ref, out_ref, *, k_steps):
    def product():
        return jnp.dot(a_ref[...], w_ref[...], preferred_element_type=F32)

    def finish(m):
        out_ref[...] = x_ref[...] + (m * _rms_scale(m)) * g_ref[...]

    if k_steps == 1:
        finish(product())
        return
    k = pl.program_id(1)

    @pl.when(k == 0)
    def _():
        out_ref[...] = product()

    @pl.when((k > 0) & (k < k_steps - 1))
    def _():
        out_ref[...] += product()

    @pl.when(k == k_steps - 1)
    def _():
        finish(out_ref[...] + product())


def _mm_norm(a, w, x, g, tm, tk):
    t, kdim = a.shape
    n = w.shape[1]
    return pl.pallas_call(
        functools.partial(_mm_norm_kernel, k_steps=kdim // tk),
        grid=(t // tm, kdim // tk),
        in_specs=[
            pl.BlockSpec((tm, tk), lambda i, k: (i, k)),
            pl.BlockSpec((tk, n), lambda i, k: (k, 0)),
            pl.BlockSpec((tm, n), lambda i, k: (i, 0)),
            pl.BlockSpec((1, n), lambda i, k: (0, 0)),
        ],
        out_specs=pl.BlockSpec((tm, n), lambda i, k: (i, 0)),
        out_shape=jax.ShapeDtypeStruct((t, n), F32),
        compiler_params=_params("arbitrary", "arbitrary"),
        name="mm_norm",
    )(a, w, x, g)


def _ffn_up_kernel(x_ref, g_ref, wa_ref, wb_ref, cw_ref, buf_ref, act_ref, nbuf_ref, h_ref, tail_ref,
                   *, seq_rows, blocks_per_seq):
    i, j = pl.program_id(0), pl.program_id(1)

    @pl.when(j == 0)
    def _():
        x = x_ref[...]
        h_ref[...] = ((x * _rms_scale(x)) * g_ref[...]).astype(BF16)

    if blocks_per_seq > 1:
        @pl.when((i == 0) & (j == 0))
        def _():
            tail_ref[...] = jnp.zeros_like(tail_ref)

    h = h_ref[...]
    a = jnp.dot(h, wa_ref[...], preferred_element_type=F32)
    b = jnp.dot(h, wb_ref[...], preferred_element_type=F32)
    tm = a.shape[0]
    row = lax.broadcasted_iota(jnp.int32, a.shape, 0)
    if seq_rows is None:
        if blocks_per_seq == 1:
            prev = buf_ref[...]
        else:
            prev = jnp.where(i % blocks_per_seq == 0, buf_ref[...], tail_ref[j])
        last = a[tm - SUBLANES:, :]
        tail_ref[j] = last
        nbuf_ref[...] = last
        p1 = prev[SUBLANES - 1:SUBLANES, :]
        p2 = prev[SUBLANES - 2:SUBLANES - 1, :]
        a1 = jnp.where(row == 0, p1, pltpu.roll(a, 1, 0))
        a2 = jnp.where(row == 0, p2, jnp.where(row == 1, p1, pltpu.roll(a, 2, 0)))
    else:
        nbuf_ref[...] = a
        in_seq = row % seq_rows
        a1 = jnp.where(in_seq == 0, buf_ref[0], pltpu.roll(a, 1, 0))
        a2 = jnp.where(in_seq < CONV_W - 1, buf_ref[1], pltpu.roll(a, 2, 0))
    cw = cw_ref[...]
    conv = a2 * cw[0:1, :] + a1 * cw[1:2, :] + a * cw[2:3, :]
    act_ref[...] = (_gelu_tanh(conv) * b).astype(act_ref.dtype)


def _ffn_up(x, g, w_up, cw, buf, seq_len, tm, tn):
    t, d = x.shape
    f = w_up.shape[1] // 2
    nj = f // tn
    if seq_len % tm == 0:
        seq_rows, bps = None, seq_len // tm
        buf_spec = pl.BlockSpec((None, SUBLANES, tn), lambda i, j: (i // bps, 0, j))
        rows_spec = pl.BlockSpec((None, SUBLANES, tn), lambda i, j: (i, 0, j))
        rows_shape = jax.ShapeDtypeStruct((t // tm, SUBLANES, f), F32)
    else:
        assert tm % seq_len == 0 and seq_len >= CONV_W - 1
        seq_rows, bps = seq_len, 1
        buf_spec = pl.BlockSpec((2, tm, tn), lambda i, j: (0, i, j))
        rows_spec = pl.BlockSpec((tm, tn), lambda i, j: (i, j))
        rows_shape = jax.ShapeDtypeStruct((t, f), F32)
    return pl.pallas_call(
        functools.partial(_ffn_up_kernel, seq_rows=seq_rows, blocks_per_seq=bps),
        grid=(t // tm, nj),
        in_specs=[
            pl.BlockSpec((tm, d), lambda i, j: (i, 0)),
            pl.BlockSpec((1, d), lambda i, j: (0, 0)),
            pl.BlockSpec((d, tn), lambda i, j: (0, j)),
            pl.BlockSpec((d, tn), lambda i, j: (0, nj + j)),
            pl.BlockSpec((SUBLANES, tn), lambda i, j: (0, j)),
            buf_spec,
        ],
        out_specs=[pl.BlockSpec((tm, tn), lambda i, j: (i, j)), rows_spec],
        out_shape=[jax.ShapeDtypeStruct((t, f), BF16), rows_shape],
        scratch_shapes=[pltpu.VMEM((tm, d), BF16), pltpu.VMEM((nj, SUBLANES, tn), F32)],
        compiler_params=_params("arbitrary", "arbitrary"),
        name="ffn_up",
    )(x, g, w_up, w_up, cw, buf)


def _pick(t, pref):
    return pref if t % pref == 0 else t


def _tile_plan(t, l, d, f):
    tm = _pick(t, 1024)
    return dict(
        tm=tm,
        tn=512,
        tn_in=1024,
        tq=1024,
        long_seq=l % tm == 0,
        s5_chunks=min(64, l // S5_CHUNK),
        tm_mm=_pick(t, 512),
        tk_out=d,
        tk_down=f // 4 if (f // 4) % LANES == 0 else 512,
    )


def _trunk(x, weights, caches):
    b, l, d = x.shape
    t = b * l
    depth = weights['w_main'].shape[0]
    n_main = weights['w_main'].shape[2]
    gate_col = n_main - 2 * d
    u_col = 3 * ATT_W
    g_s5 = weights['lam_re'].shape[1]
    f = weights['w_down'].shape[1]
    tp = _tile_plan(t, l, d, f)
    tm, tn, long_seq = tp['tm'], tp['tn'], tp['long_seq']
    hp = ATT_W // LANES
    x = x.reshape(t, d)
    st = {k: [] for k in ('k', 'v', 'logf', 're', 'im', 'conv')}
    if caches is None:
        kt_all = jnp.zeros((depth, b, ATT_W, l), F32)
        vt_all = jnp.zeros((depth, b, ATT_W, l), F32)
    for layer in range(depth):
        wl = {k: v[layer] for k, v in weights.items()}
        z, logf = _norm_in(x, wl['g_mix_pre'], wl['w_main'], wl['w_f'], wl['b_f'], tm, tp['tn_in'])
        lf = logf[:, :N_HEADS].reshape(b, l, N_HEADS)
        st['logf'].append(lf)
        z3 = z.reshape(b, l, n_main)
        if caches is None:
            c = _cumsum_rows(_lanes_by_sequence(lf, l), CUMSUM_ROWS)
            o, kt_all, vt_all = _attn_prompt(z3, c, tp['tq'], kt_all, vt_all, layer)
            h0 = jnp.zeros((b, g_s5 // S5_TILE_GROUPS, 1, S5_TILE_STATE), F32)
            assert long_seq
            buf = jnp.zeros((b, SUBLANES, f), F32)
        else:
            st['k'].append(z[:, ATT_W:2 * ATT_W].reshape(b, l, N_HEADS, HEAD_DIM))
            st['v'].append(z[:, 2 * ATT_W:3 * ATT_W].reshape(b, l, N_HEADS, HEAD_DIM))
            past = caches['logf'].shape[2]
            lf_all = jnp.concatenate([caches['logf'][layer], lf], axis=1)
            padded = -(-(past + l) // CUMSUM_ROWS) * CUMSUM_ROWS
            c_all = _cumsum_rows(_lanes_by_sequence(lf_all, padded), CUMSUM_ROWS)
            nck_all = -c_all[0, :, :b * N_HEADS].T.reshape(b, N_HEADS, padded)
            nck_c = nck_all[:, :, :past].reshape(b, hp, 2, past)
            nck_n = nck_all[:, :, past:past + l].reshape(b, hp, 2, l)
            o = _attn_sample(z3, caches['k'], caches['v'], layer, nck_c, nck_n)
            h0 = _s5_pack_state(caches['ssm_re'][layer], caches['ssm_im'][layer])
            hist = caches['conv'][layer]
            if long_seq:
                buf = jnp.pad(hist, ((0, 0), (SUBLANES - (CONV_W - 1), 0), (0, 0)))
            else:
                zero = jnp.zeros((b, l, f), F32)
                buf = jnp.stack([zero.at[:, 0].set(hist[:, 1]),
                                 zero.at[:, 0].set(hist[:, 0]).at[:, 1].set(hist[:, 1])]).reshape(2, t, f)
        o = o.reshape(t, ATT_W)
        y, s_end = _s5(z3, u_col, weights['s5_mats'][layer], h0, tp['s5_chunks'])
        y = y.reshape(t, g_s5 * S5_GROUP)
        s_re, s_im = _s5_unpack_state(s_end, g_s5)
        st['re'].append(s_re)
        st['im'].append(s_im)
        mix_in = _mix(o, y, z, gate_col, wl['w_att_proj'], wl['w_glu_v'], wl['w_glu_g'], tm, tn)
        x = _mm_norm(mix_in, wl['w_out'], x, wl['g_mix_post'], tp['tm_mm'], tp['tk_out'])
        act, rows = _ffn_up(x, wl['g_ffn_pre'], wl['w_up'], wl['conv_w'], buf, l, tm, tn)
        if long_seq:
            rows = rows.reshape(b, l // tm, SUBLANES, f)[:, -1]
        else:
            rows = rows.reshape(b, l, f)
        st['conv'].append(rows[:, -(CONV_W - 1):])
        x = _mm_norm(act, wl['w_down'], x, wl['g_ffn_post'], tp['tm_mm'], tp['tk_down'])
    stacked = [jnp.stack(st[k]) for k in ('logf', 're', 'im', 'conv')]
    if caches is None:
        kv = [a.reshape(depth, b, N_HEADS, HEAD_DIM, l).transpose(0, 1, 4, 2, 3) for a in (kt_all, vt_all)]
    else:
        kv = [jnp.stack(st['k']), jnp.stack(st['v'])]
    return x.reshape(b, l, d), kv + stacked


def kernel(x_prompt, x_sample, cache_k, cache_v, cache_logf, state_ssm_re, state_ssm_im, state_conv,
           g_mix_pre, w_in, b_f, lam_re, lam_im, log_dt, b_re, b_im, c_re, c_im, d_skip,
           w_att_proj, w_glu_v, w_glu_g, w_out, g_mix_post, g_ffn_pre, w_up, conv_w, w_down, g_ffn_post):
    depth, d, _ = w_in.shape
    s5_w = d_skip.shape[1]
    f_lo, f_hi = 3 * ATT_W, 3 * ATT_W + N_HEADS
    scale = HEAD_DIM ** -0.5
    w_main = jnp.concatenate([w_in[:, :, :ATT_W] * scale, w_in[:, :, ATT_W:f_lo], w_in[:, :, f_hi:]], axis=2)
    weights = {
        'w_main': w_main.astype(BF16),
        'w_f': jnp.pad(w_in[:, :, f_lo:f_hi], ((0, 0), (0, 0), (0, LANES - N_HEADS))).astype(BF16),
        'b_f': jnp.pad(b_f, ((0, 0), (0, LANES - N_HEADS))).reshape(depth, 1, LANES),
        'g_mix_pre': g_mix_pre.reshape(depth, 1, d),
        'g_mix_post': g_mix_post.reshape(depth, 1, d),
        'g_ffn_pre': g_ffn_pre.reshape(depth, 1, d),
        'g_ffn_post': g_ffn_post.reshape(depth, 1, d),
        'lam_re': lam_re,
        's5_mats': [_s5_matrices(lam_re[n], lam_im[n], log_dt[n], b_re[n], b_im[n], c_re[n], c_im[n], d_skip[n])
                    for n in range(depth)],
        'w_att_proj': w_att_proj.astype(BF16),
        'w_glu_v': w_glu_v.astype(BF16),
        'w_glu_g': w_glu_g.astype(BF16),
        'w_out': w_out.astype(BF16),
        'w_up': w_up.astype(BF16),
        'conv_w': jnp.pad(conv_w, ((0, 0), (0, SUBLANES - CONV_W), (0, 0))),
        'w_down': w_down.astype(BF16),
    }
    assert s5_w == lam_re.shape[1] * S5_GROUP
    sb, sp = cache_k.shape[1], cache_k.shape[2]
    caches = {
        'k': cache_k.transpose(0, 1, 3, 4, 2).reshape(depth, sb, ATT_W, sp),
        'v': cache_v.transpose(0, 1, 3, 4, 2).reshape(depth, sb, ATT_W, sp),
        'logf': cache_logf, 'ssm_re': state_ssm_re, 'ssm_im': state_ssm_im, 'conv': state_conv,
    }
    y_prompt, p_st = _trunk(x_prompt, weights, None)
    y_sample, s_st = _trunk(x_sample, weights, caches)
    return (y_prompt, y_sample, *p_st, *s_st)
```

```python
import functools

import jax
import jax.numpy as jnp
from jax import lax
from jax.experimental import pallas as pl
from jax.experimental.pallas import tpu as pltpu

F32 = jnp.float32
BF16 = jnp.bfloat16
HIGHEST = lax.Precision.HIGHEST

RMS_EPS = 1e-6
N_HEADS = 16
HEAD_DIM = 64
ATT_W = N_HEADS * HEAD_DIM
S5_GROUP = 16
S5_STATE = 64
CONV_W = 3
LANES = 128
SUBLANES = 8
S5_CHUNK = 16
VMEM_LIMIT_BYTES = 56 * 1024 * 1024


def _params(*sem):
    return pltpu.CompilerParams(dimension_semantics=sem, vmem_limit_bytes=VMEM_LIMIT_BYTES)


def _gelu_tanh(x):
    return 0.5 * x * (1.0 + jnp.tanh(0.7978845608028654 * (x + 0.044715 * (x * x * x))))


def _sigmoid(x):
    return 0.5 * jnp.tanh(0.5 * x) + 0.5


def _log_sigmoid(x):
    return jnp.minimum(x, 0.0) - jnp.log1p(jnp.exp(-jnp.abs(x)))


def _rms_scale(x):
    return lax.rsqrt(jnp.mean(x * x, axis=-1, keepdims=True) + RMS_EPS)


def _norm_in_kernel(x_ref, g_ref, w_ref, wf_ref, bf_ref, z_ref, lf_ref, h_ref):
    @pl.when(pl.program_id(1) == 0)
    def _():
        x = x_ref[...]
        h = ((x * _rms_scale(x)) * g_ref[...]).astype(BF16)
        h_ref[...] = h
        fl = jnp.dot(h, wf_ref[...], preferred_element_type=F32) + bf_ref[...]
        lf_ref[...] = _log_sigmoid(fl)

    z_ref[...] = jnp.dot(h_ref[...], w_ref[...], preferred_element_type=F32)


def _norm_in(x, g, w, wf, bf, tm, tn):
    t, d = x.shape
    n = w.shape[1]
    return pl.pallas_call(
        _norm_in_kernel,
        grid=(t // tm, n // tn),
        in_specs=[
            pl.BlockSpec((tm, d), lambda i, j: (i, 0)),
            pl.BlockSpec((1, d), lambda i, j: (0, 0)),
            pl.BlockSpec((d, tn), lambda i, j: (0, j)),
            pl.BlockSpec((d, LANES), lambda i, j: (0, 0)),
            pl.BlockSpec((1, LANES), lambda i, j: (0, 0)),
        ],
        out_specs=[
            pl.BlockSpec((tm, tn), lambda i, j: (i, j)),
            pl.BlockSpec((tm, LANES), lambda i, j: (i, 0)),
        ],
        out_shape=[jax.ShapeDtypeStruct((t, n), F32), jax.ShapeDtypeStruct((t, LANES), F32)],
        scratch_shapes=[pltpu.VMEM((tm, d), BF16)],
        compiler_params=_params("arbitrary", "arbitrary"),
        name="norm_in",
    )(x, g, w, wf, bf)


CUMSUM_ROWS = 512


def _cumsum_rows_kernel(x_ref, o_ref, carry_ref):
    @pl.when(pl.program_id(1) == 0)
    def _():
        carry_ref[...] = jnp.zeros_like(carry_ref)

    n = x_ref.shape[0]
    row = lax.broadcasted_iota(jnp.int32, (n, n), 0)
    col = lax.broadcasted_iota(jnp.int32, (n, n), 1)
    lower = (col <= row).astype(F32)
    c = jnp.dot(lower, x_ref[...], precision=HIGHEST, preferred_element_type=F32) + carry_ref[...]
    o_ref[...] = c
    carry_ref[...] = c[n - 1:n, :]


def _lanes_by_sequence(lf, rows):
    b, l, h = lf.shape
    lanes = -(-(b * h) // LANES) * LANES
    x = lf.transpose(1, 0, 2).reshape(l, b * h)
    return jnp.pad(x, ((0, rows - l), (0, lanes - b * h)))[None]


def _cumsum_rows(x, blk):
    b, l, w = x.shape
    return pl.pallas_call(
        _cumsum_rows_kernel,
        grid=(b, l // blk),
        in_specs=[pl.BlockSpec((None, blk, w), lambda i, j: (i, j, 0))],
        out_specs=pl.BlockSpec((None, blk, w), lambda i, j: (i, j, 0)),
        out_shape=jax.ShapeDtypeStruct((b, l, w), F32),
        scratch_shapes=[pltpu.VMEM((1, w), F32)],
        compiler_params=_params("arbitrary", "arbitrary"),
        name="cumsum_rows",
    )(x)


def _head_masks():
    lane = lax.broadcasted_iota(jnp.int32, (1, LANES), 1)
    return lane < HEAD_DIM, lane >= HEAD_DIM


BIAS_PARTS = 3
V_ROWS = HEAD_DIM + 16

def _attn_kernel(q_ref, k_ref, v_ref, c_ref, kt_all_ref, vt_all_ref, o_ref, kt_ref, vto_ref,
                 kx_ref, vt_ref, m_ref, acc_ref, sa_ref, sb_ref, *, tq, tk):
    del kt_all_ref, vt_all_ref
    seq = pl.program_id(0)
    hp = pl.program_id(1)
    i = pl.program_id(2)
    n_blk = k_ref.shape[0] // tk
    lane = lax.broadcasted_iota(jnp.int32, (1, LANES), 1)
    own = [(lane >= h * HEAD_DIM) & (lane < (h + 1) * HEAD_DIM) for h in range(2)]
    spare = [(1 - h) * HEAD_DIM for h in range(2)]

    @pl.when(i == 0)
    def _():
        er = lax.broadcasted_iota(jnp.int32, (LANES, LANES), 0)
        ec = lax.broadcasted_iota(jnp.int32, (LANES, LANES), 1)
        row = lax.broadcasted_iota(jnp.int32, (LANES, tk), 0)

        def prep(j, carry):
            r0 = pl.multiple_of(j * tk, tk)
            kblk = k_ref[pl.ds(r0, tk), :]
            nc = -c_ref[pl.ds(r0, tk), :]
            parts = []
            rem = nc
            for _ in range(BIAS_PARTS):
                piece = rem.astype(BF16)
                parts.append(piece)
                rem = rem - piece.astype(F32)
            vt = v_ref[pl.ds(r0, tk), :].T
            for h in range(2):
                bias = jnp.zeros((tk, LANES), F32)
                for n, piece in enumerate(parts):
                    sel = ((er == seq * N_HEADS + 2 * hp + h) & (ec == spare[h] + n)).astype(BF16)
                    bias = bias + jnp.dot(piece, sel, preferred_element_type=F32)
                kx_ref[h, pl.ds(r0, tk), :] = (jnp.where(own[h], kblk, 0.0) + bias).astype(BF16)
                vth = vt if h == 0 else pltpu.roll(vt, HEAD_DIM, 0)
                vt_ref[h, j] = jnp.where(row < HEAD_DIM, vth, 1.0)[:V_ROWS].astype(BF16)
            return carry

        lax.fori_loop(0, n_blk, prep, 0)

    own_rows = pl.ds(pl.multiple_of(i * tq, tq), tq)
    kt_ref[...] = k_ref[own_rows, :].T
    vto_ref[...] = v_ref[own_rows, :].T

    q = q_ref[...]
    qx = []
    for h in range(2):
        ones = (lane >= spare[h]) & (lane < spare[h] + BIAS_PARTS)
        qx.append((jnp.where(own[h], q, 0.0) + jnp.where(ones, 1.0, 0.0)).astype(BF16))
    m_ref[...] = jnp.full_like(m_ref, -jnp.inf)
    acc_ref[...] = jnp.zeros_like(acc_ref)
    nt_dims = (((1,), (1,)), ((), ()))

    def scores(kb, s_ref, q_from=0):
        start = pl.multiple_of(kb * tk, tk)
        for h in range(2):
            s_ref[h, :, q_from:] = lax.dot_general(kx_ref[h, pl.ds(start, tk), :], qx[h][q_from:], nt_dims,
                                                   preferred_element_type=F32)

    def absorb(kb, s_ref, first_key=None, q_from=0):
        nq = tq - q_from
        sts = [s_ref[h, :, q_from:] for h in range(2)]
        if first_key is not None:
            key = lax.broadcasted_iota(jnp.int32, (tk, nq), 0) + first_key
            qry = lax.broadcasted_iota(jnp.int32, (tk, nq), 1) + q_from
            sts = [jnp.where(key <= qry, st, -jnp.inf) for st in sts]
        m_old = [m_ref[h, :, q_from:] for h in range(2)]
        m_new = [jnp.maximum(m_old[h], jnp.max(sts[h], axis=0, keepdims=True)) for h in range(2)]
        pts = [jnp.exp(sts[h] - m_new[h]).astype(BF16) for h in range(2)]
        pvs = [jnp.dot(vt_ref[h, kb], pts[h], preferred_element_type=F32) for h in range(2)]
        for h in range(2):
            acc_ref[h, :, q_from:] = jnp.exp(m_old[h] - m_new[h]) * acc_ref[h, :, q_from:] + pvs[h]
            m_ref[h, :, q_from:] = m_new[h]

    scores(0, sa_ref)

    def pair(p, carry):
        kb = 2 * p
        scores(kb + 1, sb_ref)
        absorb(kb, sa_ref)
        scores(kb + 2, sa_ref)
        absorb(kb + 1, sb_ref)
        return carry

    lax.fori_loop(0, i, pair, 0)
    scores(2 * i + 1, sb_ref, q_from=tk)
    absorb(2 * i, sa_ref, first_key=0)
    absorb(2 * i + 1, sb_ref, first_key=tk, q_from=tk)

    outs = []
    for h in range(2):
        a = acc_ref[h]
        outs.append(a[:HEAD_DIM] * (1.0 / a[HEAD_DIM:HEAD_DIM + 1]))
    o_ref[...] = jnp.concatenate(outs, axis=0).T.astype(o_ref.dtype)


def _attn_prompt(z, c, tq, kt_all, vt_all, layer):
    b, l, _ = z.shape
    assert b * N_HEADS <= LANES
    hp = ATT_W // LANES
    tk = tq // 2
    return pl.pallas_call(
        functools.partial(_attn_kernel, tq=tq, tk=tk),
        grid=(b, hp, l // tq),
        in_specs=[
            pl.BlockSpec((None, tq, LANES), lambda bi, h, i: (bi, i, h)),
            pl.BlockSpec((None, l, LANES), lambda bi, h, i: (bi, 0, hp + h)),
            pl.BlockSpec((None, l, LANES), lambda bi, h, i: (bi, 0, 2 * hp + h)),
            pl.BlockSpec((None, l, LANES), lambda bi, h, i: (0, 0, 0)),
            pl.BlockSpec(memory_space=pl.ANY),
            pl.BlockSpec(memory_space=pl.ANY),
        ],
        out_specs=[
            pl.BlockSpec((None, tq, LANES), lambda bi, h, i: (bi, i, h)),
            pl.BlockSpec((None, None, LANES, tq), lambda bi, h, i: (layer, bi, h, i)),
            pl.BlockSpec((None, None, LANES, tq), lambda bi, h, i: (layer, bi, h, i)),
        ],
        out_shape=[
            jax.ShapeDtypeStruct((b, l, ATT_W), BF16),
            jax.ShapeDtypeStruct(kt_all.shape, F32),
            jax.ShapeDtypeStruct(vt_all.shape, F32),
        ],
        input_output_aliases={4: 1, 5: 2},
        scratch_shapes=[
            pltpu.VMEM((2, l, LANES), BF16),
            pltpu.VMEM((2, l // tk, V_ROWS, tk), BF16),
            pltpu.VMEM((2, 1, tq), F32),
            pltpu.VMEM((2, V_ROWS, tq), F32),
            pltpu.VMEM((2, tk, tq), F32),
            pltpu.VMEM((2, tk, tq), F32),
        ],
        compiler_params=_params("arbitrary", "arbitrary", "arbitrary"),
        name="attn_prompt",
    )(z, z, z, c, kt_all, vt_all)


def _attn_cache_kernel(q_ref, kn_ref, vn_ref, kc_ref, vc_ref, nckc_ref, nckn_ref, o_ref):
    q = q_ref[...]
    lq = q.shape[0]
    masks = _head_masks()
    kc = kc_ref[...].astype(BF16)
    vc = vc_ref[...].astype(BF16)
    kn = kn_ref[...].astype(BF16)
    vn = vn_ref[...].astype(BF16)
    nt_dims = (((1,), (1,)), ((), ()))
    r = lax.broadcasted_iota(jnp.int32, (lq, lq), 0)
    c = lax.broadcasted_iota(jnp.int32, (lq, lq), 1)
    outs = []
    for h in range(2):
        qh = jnp.where(masks[h], q, 0.0).astype(BF16)
        sc = jnp.dot(qh, kc, preferred_element_type=F32) + nckc_ref[pl.ds(h, 1), :]
        sn = lax.dot_general(qh, kn, nt_dims, preferred_element_type=F32) + nckn_ref[pl.ds(h, 1), :]
        sn = jnp.where(c <= r, sn, -jnp.inf)
        m = jnp.maximum(jnp.max(sc, axis=-1, keepdims=True), jnp.max(sn, axis=-1, keepdims=True))
        pc = jnp.exp(sc - m)
        pn = jnp.exp(sn - m)
        den = jnp.sum(pc, axis=-1, keepdims=True) + jnp.sum(pn, axis=-1, keepdims=True)
        acc = (lax.dot_general(pc.astype(BF16), vc, nt_dims, preferred_element_type=F32)
               + jnp.dot(pn.astype(BF16), vn, preferred_element_type=F32))
        outs.append(acc / den)
    o_ref[...] = jnp.where(masks[0], outs[0], outs[1]).astype(o_ref.dtype)


def _attn_sample(z, cache_k, cache_v, layer, nck_c, nck_n):
    b, lq, _ = z.shape
    p = cache_k.shape[3]
    hp = ATT_W // LANES
    return pl.pallas_call(
        _attn_cache_kernel,
        grid=(b, hp),
        in_specs=[
            pl.BlockSpec((None, lq, LANES), lambda bi, h: (bi, 0, h)),
            pl.BlockSpec((None, lq, LANES), lambda bi, h: (bi, 0, hp + h)),
            pl.BlockSpec((None, lq, LANES), lambda bi, h: (bi, 0, 2 * hp + h)),
            pl.BlockSpec((None, None, LANES, p), lambda bi, h: (layer, bi, h, 0)),
            pl.BlockSpec((None, None, LANES, p), lambda bi, h: (layer, bi, h, 0)),
            pl.BlockSpec((None, None, 2, p), lambda bi, h: (bi, h, 0, 0)),
            pl.BlockSpec((None, None, 2, lq), lambda bi, h: (bi, h, 0, 0)),
        ],
        out_specs=pl.BlockSpec((None, lq, LANES), lambda bi, h: (bi, 0, h)),
        out_shape=jax.ShapeDtypeStruct((b, lq, ATT_W), BF16),
        compiler_params=_params("arbitrary", "arbitrary"),
        name="attn_sample",
    )(z, z, z, cache_k, cache_v, nck_c, nck_n)


S5_TILE_GROUPS = LANES // S5_GROUP
S5_TILE_STATE = 2 * S5_TILE_GROUPS * S5_STATE


def _s5_kernel(u_ref, bw_ref, bd_ref, bv_ref, are_ref, aim_ref, d_ref, h0_ref,
               y_ref, send_ref, s_ref, e_ref, sp_ref, *, nb, cb):
    t = pl.program_id(1)
    tc = S5_CHUNK
    rows = nb * cb

    @pl.when(t == 0)
    def _():
        s_ref[...] = h0_ref[...]

    def tokens(j):
        if cb == 1:
            return u_ref[:, j, :]
        return u_ref[:, pl.ds(j, cb, stride=tc), :].reshape(rows, LANES)

    us = [tokens(j) for j in range(tc)]
    ucat = jnp.concatenate([u.astype(BF16) for u in us], axis=1)
    e = jnp.dot(ucat, bw_ref[...], preferred_element_type=F32)
    e_ref[...] = e.reshape(nb, cb, S5_TILE_STATE)

    a_re = are_ref[...]
    a_im = aim_ref[...]

    def advance(c, carry):
        for b in range(nb):
            s = s_ref[b]
            sp_ref[b, pl.ds(c, 1), :] = s
            s_ref[b] = a_re * s + a_im * pltpu.roll(s, S5_TILE_STATE // 2, 1) + e_ref[b, pl.ds(c, 1), :]
        return carry

    lax.fori_loop(0, cb, advance, 0)

    sp = sp_ref[...].reshape(rows, S5_TILE_STATE).astype(BF16)
    d = d_ref[...]
    y_state = jnp.dot(sp, bv_ref[...], preferred_element_type=F32)
    pair = 2 * LANES
    for k in range(tc // 2):
        kk = pair * (k + 1)
        yk = jnp.dot(ucat[:, :kk], bd_ref[:kk, k * pair:(k + 1) * pair], preferred_element_type=F32)
        yk = yk + y_state[:, k * pair:(k + 1) * pair]
        for jj in range(2):
            jo = 2 * k + jj
            y = yk[:, jj * LANES:(jj + 1) * LANES] + us[jo] * d
            if cb == 1:
                y_ref[:, jo, :] = y
            else:
                y_ref[:, pl.ds(jo, cb, stride=tc), :] = y.reshape(nb, cb, LANES)

    @pl.when(t == pl.num_programs(1) - 1)
    def _():
        send_ref[...] = s_ref[...]


def _s5(z3, u_col, mats, h0, cb):
    bw, bd, bv, a_re, a_im, d = mats
    nb, l, _ = z3.shape
    tiles = bw.shape[0]
    tc = S5_CHUNK
    blk = tc * cb
    col0 = u_col // LANES
    return pl.pallas_call(
        functools.partial(_s5_kernel, nb=nb, cb=cb),
        grid=(tiles, l // blk),
        in_specs=[
            pl.BlockSpec((nb, blk, LANES), lambda g, t: (0, t, col0 + g)),
            pl.BlockSpec((None, tc * LANES, S5_TILE_STATE), lambda g, t: (g, 0, 0)),
            pl.BlockSpec((None, tc * LANES, tc * LANES), lambda g, t: (g, 0, 0)),
            pl.BlockSpec((None, S5_TILE_STATE, tc * LANES), lambda g, t: (g, 0, 0)),
            pl.BlockSpec((None, 1, S5_TILE_STATE), lambda g, t: (g, 0, 0)),
            pl.BlockSpec((None, 1, S5_TILE_STATE), lambda g, t: (g, 0, 0)),
            pl.BlockSpec((None, 1, LANES), lambda g, t: (g, 0, 0)),
            pl.BlockSpec((nb, None, 1, S5_TILE_STATE), lambda g, t: (0, g, 0, 0)),
        ],
        out_specs=[
            pl.BlockSpec((nb, blk, LANES), lambda g, t: (0, t, g)),
            pl.BlockSpec((nb, None, 1, S5_TILE_STATE), lambda g, t: (0, g, 0, 0)),
        ],
        out_shape=[jax.ShapeDtypeStruct((nb, l, tiles * LANES), F32), jax.ShapeDtypeStruct(h0.shape, F32)],
        scratch_shapes=[
            pltpu.VMEM((nb, 1, S5_TILE_STATE), F32),
            pltpu.VMEM((nb, cb, S5_TILE_STATE), F32),
            pltpu.VMEM((nb, cb, S5_TILE_STATE), F32),
        ],
        compiler_params=_params("arbitrary", "arbitrary"),
        name="s5",
    )(z3, bw, bd, bv, a_re, a_im, d, h0)


def _s5_matrices(lam_re, lam_im, log_dt, b_re, b_im, c_re, c_im, d_skip):
    g, p = lam_re.shape
    tc = S5_CHUNK
    dt = jnp.exp(log_dt)[:, None]
    k = jnp.arange(tc + 1, dtype=F32)[:, None, None]
    mag = jnp.exp(lam_re * dt * k)
    pw_re = mag * jnp.cos(lam_im * dt * k)
    pw_im = mag * jnp.sin(lam_im * dt * k)
    den = lam_re * lam_re + lam_im * lam_im
    nr = pw_re[1] - 1.0
    f_re = (nr * lam_re + pw_im[1] * lam_im) / den
    f_im = (pw_im[1] * lam_re - nr * lam_im) / den
    bb_re = f_re[..., None] * b_re - f_im[..., None] * b_im
    bb_im = f_re[..., None] * b_im + f_im[..., None] * b_re
    ein = functools.partial(jnp.einsum, precision=HIGHEST)
    cp_re = c_re[None] * pw_re[:tc, :, None, :] - c_im[None] * pw_im[:tc, :, None, :]
    cp_im = c_re[None] * pw_im[:tc, :, None, :] + c_im[None] * pw_re[:tc, :, None, :]
    taps = ein('tghp,gpk->tghk', cp_re, bb_re) - ein('tghp,gpk->tghk', cp_im, bb_im)
    k_rev = (tc - 1) - jnp.arange(tc, dtype=F32)[:, None, None]
    mag_rev = jnp.exp(lam_re * dt * k_rev)
    rp_re = (mag_rev * jnp.cos(lam_im * dt * k_rev))[:, :, :, None]
    rp_im = (mag_rev * jnp.sin(lam_im * dt * k_rev))[:, :, :, None]
    w_re = rp_re * bb_re[None] - rp_im * bb_im[None]
    w_im = rp_re * bb_im[None] + rp_im * bb_re[None]
    z_re = c_re[None] * pw_re[1:, :, None, :] - c_im[None] * pw_im[1:, :, None, :]
    z_im = c_re[None] * pw_im[1:, :, None, :] + c_im[None] * pw_re[1:, :, None, :]
    gl = S5_TILE_GROUPS
    tiles = g // gl
    h = S5_GROUP
    tok_lane = jnp.arange(tc * LANES)
    st_lane = jnp.arange(S5_TILE_STATE)
    tok_group = (tok_lane // h) % gl
    st_group = (st_lane // p) % gl
    cmp_lane = jnp.arange(2 * p)
    rep_state = ((st_lane[None, :] // (gl * p) == cmp_lane[:, None] // p)
                 & (st_lane[None, :] % p == cmp_lane[:, None] % p)).astype(BF16)

    def expand(values, rep, row_group, col_group, rep_rows=False):
        spec = 'rq,tqc->trc' if rep_rows else 'trq,qc->trc'
        args = (rep.T, values.astype(BF16)) if rep_rows else (values.astype(BF16), rep)
        wide = jnp.einsum(spec, *args, preferred_element_type=F32)
        return jnp.where(row_group[:, None] == col_group[None, :], wide, 0.0).astype(BF16)

    tp = taps.reshape(tc, tiles, gl, h, h).transpose(1, 0, 2, 4, 3)
    eye = jnp.eye(gl, dtype=F32)
    bd = (tp[:, :, :, :, None, :] * eye[None, None, :, None, :, None]).reshape(tiles, tc, LANES, LANES).astype(BF16)
    lag = jnp.arange(tc)[None, :] - jnp.arange(tc)[:, None]
    bd = jnp.where((lag >= 0)[None, :, :, None, None], bd[:, jnp.clip(lag, 0, tc - 1)], 0)
    bd = bd.transpose(0, 1, 3, 2, 4).reshape(tiles, tc * LANES, tc * LANES)
    ws = jnp.stack([w_re, w_im]).reshape(2, tc, tiles, gl, p, h).transpose(2, 1, 3, 5, 0, 4)
    bw = expand(ws.reshape(tiles, tc * LANES, 2 * p), rep_state, tok_group, st_group)
    zs = jnp.stack([z_re, -z_im]).reshape(2, tc, tiles, gl, h, p).transpose(2, 0, 5, 1, 3, 4)
    bv = expand(zs.reshape(tiles, 2 * p, tc * LANES), rep_state, st_group, tok_group, rep_rows=True)
    ar = pw_re[tc].reshape(tiles, gl * p)
    ai = pw_im[tc].reshape(tiles, gl * p)
    a_re = jnp.concatenate([ar, ar], axis=-1)[:, None, :]
    a_im = jnp.concatenate([-ai, ai], axis=-1)[:, None, :]
    d = d_skip.reshape(tiles, 1, LANES)
    return bw.astype(BF16), bd.astype(BF16), bv.astype(BF16), a_re, a_im, d


def _s5_pack_state(re, im):
    b, g, p = re.shape
    tiles = g // S5_TILE_GROUPS
    return jnp.concatenate([re.reshape(b, tiles, 1, -1), im.reshape(b, tiles, 1, -1)], axis=-1)


def _s5_unpack_state(s, g):
    b = s.shape[0]
    half = S5_TILE_STATE // 2
    return s[..., :half].reshape(b, g, S5_STATE), s[..., half:].reshape(b, g, S5_STATE)


def _mix_kernel(o_ref, y_ref, ga_ref, gb_ref, wa_ref, wv_ref, wg_ref, out_ref, g5_ref):
    @pl.when(pl.program_id(1) == 0)
    def _():
        g5_ref[...] = _gelu_tanh(y_ref[...].astype(F32)).astype(BF16)

    ya = jnp.dot(o_ref[...], wa_ref[...], preferred_element_type=F32)
    g5 = g5_ref[...]
    yb = (jnp.dot(g5, wv_ref[...], preferred_element_type=F32)
          * _sigmoid(jnp.dot(g5, wg_ref[...], preferred_element_type=F32)))
    out = _sigmoid(ga_ref[...]) * ya + _sigmoid(gb_ref[...]) * yb
    out_ref[...] = out.astype(out_ref.dtype)


def _mix(o, y, z, gate_col, wa, wv, wg, tm, tn):
    t, kdim = o.shape
    n = wa.shape[1]
    ga0 = gate_col // tn
    gb0 = (gate_col + n) // tn
    return pl.pallas_call(
        _mix_kernel,
        grid=(t // tm, n // tn),
        in_specs=[
            pl.BlockSpec((tm, kdim), lambda i, j: (i, 0)),
            pl.BlockSpec((tm, kdim), lambda i, j: (i, 0)),
            pl.BlockSpec((tm, tn), lambda i, j: (i, ga0 + j)),
            pl.BlockSpec((tm, tn), lambda i, j: (i, gb0 + j)),
            pl.BlockSpec((kdim, tn), lambda i, j: (0, j)),
            pl.BlockSpec((kdim, tn), lambda i, j: (0, j)),
            pl.BlockSpec((kdim, tn), lambda i, j: (0, j)),
        ],
        out_specs=pl.BlockSpec((tm, tn), lambda i, j: (i, j)),
        out_shape=jax.ShapeDtypeStruct((t, n), BF16),
        scratch_shapes=[pltpu.VMEM((tm, kdim), BF16)],
        compiler_params=_params("arbitrary", "arbitrary"),
        name="mix",
    )(o, y, z, z, wa, wv, wg)


def _mm_norm_kernel(a_ref, w_ref, x_ref, g_ref, out_ref, *, k_steps):
    def product():
        return jnp.dot(a_ref[...], w_ref[...], preferred_element_type=F32)

    def finish(m):
        out_ref[...] = x_ref[...] + (m * _rms_scale(m)) * g_ref[...]

    if k_steps == 1:
        finish(product())
        return
    k = pl.program_id(1)

    @pl.when(k == 0)
    def _():
        out_ref[...] = product()

    @pl.when((k > 0) & (k < k_steps - 1))
    def _():
        out_ref[...] += product()

    @pl.when(k == k_steps - 1)
    def _():
        finish(out_ref[...] + product())


def _mm_norm(a, w, x, g, tm, tk):
    t, kdim = a.shape
    n = w.shape[1]
    return pl.pallas_call(
        functools.partial(_mm_norm_kernel, k_steps=kdim // tk),
        grid=(t // tm, kdim // tk),
        in_specs=[
            pl.BlockSpec((tm, tk), lambda i, k: (i, k)),
            pl.BlockSpec((tk, n), lambda i, k: (k, 0)),
            pl.BlockSpec((tm, n), lambda i, k: (i, 0)),
            pl.BlockSpec((1, n), lambda i, k: (0, 0)),
        ],
        out_specs=pl.BlockSpec((tm, n), lambda i, k: (i, 0)),
        out_shape=jax.ShapeDtypeStruct((t, n), F32),
        compiler_params=_params("arbitrary", "arbitrary"),
        name="mm_norm",
    )(a, w, x, g)


def _ffn_up_kernel(x_ref, g_ref, wa_ref, wb_ref, cw_ref, buf_ref, act_ref, nbuf_ref, h_ref, tail_ref,
                   *, seq_rows, blocks_per_seq):
    i, j = pl.program_id(0), pl.program_id(1)

    @pl.when(j == 0)
    def _():
        x = x_ref[...]
        h_ref[...] = ((x * _rms_scale(x)) * g_ref[...]).astype(BF16)

    if blocks_per_seq > 1:
        @pl.when((i == 0) & (j == 0))
        def _():
            tail_ref[...] = jnp.zeros_like(tail_ref)

    h = h_ref[...]
    a = jnp.dot(h, wa_ref[...], preferred_element_type=F32)
    b = jnp.dot(h, wb_ref[...], preferred_element_type=F32)
    tm = a.shape[0]
    row = lax.broadcasted_iota(jnp.int32, a.shape, 0)
    if seq_rows is None:
        if blocks_per_seq == 1:
            prev = buf_ref[...]
        else:
            prev = jnp.where(i % blocks_per_seq == 0, buf_ref[...], tail_ref[j])
        last = a[tm - SUBLANES:, :]
        tail_ref[j] = last
        nbuf_ref[...] = last
        p1 = prev[SUBLANES - 1:SUBLANES, :]
        p2 = prev[SUBLANES - 2:SUBLANES - 1, :]
        a1 = jnp.where(row == 0, p1, pltpu.roll(a, 1, 0))
        a2 = jnp.where(row == 0, p2, jnp.where(row == 1, p1, pltpu.roll(a, 2, 0)))
    else:
        nbuf_ref[...] = a
        in_seq = row % seq_rows
        a1 = jnp.where(in_seq == 0, buf_ref[0], pltpu.roll(a, 1, 0))
        a2 = jnp.where(in_seq < CONV_W - 1, buf_ref[1], pltpu.roll(a, 2, 0))
    cw = cw_ref[...]
    conv = a2 * cw[0:1, :] + a1 * cw[1:2, :] + a * cw[2:3, :]
    act_ref[...] = (_gelu_tanh(conv) * b).astype(act_ref.dtype)


def _ffn_up(x, g, w_up, cw, buf, seq_len, tm, tn):
    t, d = x.shape
    f = w_up.shape[1] // 2
    nj = f // tn
    if seq_len % tm == 0:
        seq_rows, bps = None, seq_len // tm
        buf_spec = pl.BlockSpec((None, SUBLANES, tn), lambda i, j: (i // bps, 0, j))
        rows_spec = pl.BlockSpec((None, SUBLANES, tn), lambda i, j: (i, 0, j))
        rows_shape = jax.ShapeDtypeStruct((t // tm, SUBLANES, f), F32)
    else:
        assert tm % seq_len == 0 and seq_len >= CONV_W - 1
        seq_rows, bps = seq_len, 1
        buf_spec = pl.BlockSpec((2, tm, tn), lambda i, j: (0, i, j))
        rows_spec = pl.BlockSpec((tm, tn), lambda i, j: (i, j))
        rows_shape = jax.ShapeDtypeStruct((t, f), F32)
    return pl.pallas_call(
        functools.partial(_ffn_up_kernel, seq_rows=seq_rows, blocks_per_seq=bps),
        grid=(t // tm, nj),
        in_specs=[
            pl.BlockSpec((tm, d), lambda i, j: (i, 0)),
            pl.BlockSpec((1, d), lambda i, j: (0, 0)),
            pl.BlockSpec((d, tn), lambda i, j: (0, j)),
            pl.BlockSpec((d, tn), lambda i, j: (0, nj + j)),
            pl.BlockSpec((SUBLANES, tn), lambda i, j: (0, j)),
            buf_spec,
        ],
        out_specs=[pl.BlockSpec((tm, tn), lambda i, j: (i, j)), rows_spec],
        out_shape=[jax.ShapeDtypeStruct((t, f), BF16), rows_shape],
        scratch_shapes=[pltpu.VMEM((tm, d), BF16), pltpu.VMEM((nj, SUBLANES, tn), F32)],
        compiler_params=_params("arbitrary", "arbitrary"),
        name="ffn_up",
    )(x, g, w_up, w_up, cw, buf)


def _pick(t, pref):
    return pref if t % pref == 0 else t


def _tile_plan(t, l, d, f):
    tm = _pick(t, 1024)
    return dict(
        tm=tm,
        tn=512,
        tn_in=1024,
        tq=1024,
        long_seq=l % tm == 0,
        s5_chunks=min(64, l // S5_CHUNK),
        tm_mm=_pick(t, 512),
        tk_out=d,
        tk_down=f // 2 if (f // 2) % (2 * LANES) == 0 else 512,
    )


def _trunk(x, weights, caches):
    b, l, d = x.shape
    t = b * l
    depth = weights['w_main'].shape[0]
    n_main = weights['w_main'].shape[2]
    gate_col = n_main - 2 * d
    u_col = 3 * ATT_W
    g_s5 = weights['lam_re'].shape[1]
    f = weights['w_down'].shape[1]
    tp = _tile_plan(t, l, d, f)
    tm, tn, long_seq = tp['tm'], tp['tn'], tp['long_seq']
    hp = ATT_W // LANES
    x = x.reshape(t, d)
    st = {k: [] for k in ('k', 'v', 'logf', 're', 'im', 'conv')}
    if caches is None:
        kt_all = jnp.zeros((depth, b, ATT_W, l), F32)
        vt_all = jnp.zeros((depth, b, ATT_W, l), F32)
    for layer in range(depth):
        wl = {k: v[layer] for k, v in weights.items()}
        z, logf = _norm_in(x, wl['g_mix_pre'], wl['w_main'], wl['w_f'], wl['b_f'], tm, tp['tn_in'])
        lf = logf[:, :N_HEADS].reshape(b, l, N_HEADS)
        st['logf'].append(lf)
        z3 = z.reshape(b, l, n_main)
        if caches is None:
            c = _cumsum_rows(_lanes_by_sequence(lf, l), CUMSUM_ROWS)
            o, kt_all, vt_all = _attn_prompt(z3, c, tp['tq'], kt_all, vt_all, layer)
            h0 = jnp.zeros((b, g_s5 // S5_TILE_GROUPS, 1, S5_TILE_STATE), F32)
            assert long_seq
            buf = jnp.zeros((b, SUBLANES, f), F32)
        else:
            st['k'].append(z[:, ATT_W:2 * ATT_W].reshape(b, l, N_HEADS, HEAD_DIM))
            st['v'].append(z[:, 2 * ATT_W:3 * ATT_W].reshape(b, l, N_HEADS, HEAD_DIM))
            past = caches['logf'].shape[2]
            lf_all = jnp.concatenate([caches['logf'][layer], lf], axis=1)
            padded = -(-(past + l) // CUMSUM_ROWS) * CUMSUM_ROWS
            c_all = _cumsum_rows(_lanes_by_sequence(lf_all, padded), CUMSUM_ROWS)
            nck_all = -c_all[0, :, :b * N_HEADS].T.reshape(b, N_HEADS, padded)
            nck_c = nck_all[:, :, :past].reshape(b, hp, 2, past)
            nck_n = nck_all[:, :, past:past + l].reshape(b, hp, 2, l)
            o = _attn_sample(z3, caches['k'], caches['v'], layer, nck_c, nck_n)
            h0 = _s5_pack_state(caches['ssm_re'][layer], caches['ssm_im'][layer])
            hist = caches['conv'][layer]
            if long_seq:
                buf = jnp.pad(hist, ((0, 0), (SUBLANES - (CONV_W - 1), 0), (0, 0)))
            else:
                zero = jnp.zeros((b, l, f), F32)
                buf = jnp.stack([zero.at[:, 0].set(hist[:, 1]),
                                 zero.at[:, 0].set(hist[:, 0]).at[:, 1].set(hist[:, 1])]).reshape(2, t, f)
        o = o.reshape(t, ATT_W)
        y, s_end = _s5(z3, u_col, weights['s5_mats'][layer], h0, tp['s5_chunks'])
        y = y.reshape(t, g_s5 * S5_GROUP)
        s_re, s_im = _s5_unpack_state(s_end, g_s5)
        st['re'].append(s_re)
        st['im'].append(s_im)
        mix_in = _mix(o, y, z, gate_col, wl['w_att_proj'], wl['w_glu_v'], wl['w_glu_g'], tm, tn)
        x = _mm_norm(mix_in, wl['w_out'], x, wl['g_mix_post'], tp['tm_mm'], tp['tk_out'])
        act, rows = _ffn_up(x, wl['g_ffn_pre'], wl['w_up'], wl['conv_w'], buf, l, tm, tn)
        if long_seq:
            rows = rows.reshape(b, l // tm, SUBLANES, f)[:, -1]
        else:
            rows = rows.reshape(b, l, f)
        st['conv'].append(rows[:, -(CONV_W - 1):])
        x = _mm_norm(act, wl['w_down'], x, wl['g_ffn_post'], tp['tm_mm'], tp['tk_down'])
    stacked = [jnp.stack(st[k]) for k in ('logf', 're', 'im', 'conv')]
    if caches is None:
        kv = [a.reshape(depth, b, N_HEADS, HEAD_DIM, l).transpose(0, 1, 4, 2, 3) for a in (kt_all, vt_all)]
    else:
        kv = [jnp.stack(st['k']), jnp.stack(st['v'])]
    return x.reshape(b, l, d), kv + stacked


def kernel(x_prompt, x_sample, cache_k, cache_v, cache_logf, state_ssm_re, state_ssm_im, state_conv,
           g_mix_pre, w_in, b_f, lam_re, lam_im, log_dt, b_re, b_im, c_re, c_im, d_skip,
           w_att_proj, w_glu_v, w_glu_g, w_out, g_mix_post, g_ffn_pre, w_up, conv_w, w_down, g_ffn_post):
    depth, d, _ = w_in.shape
    s5_w = d_skip.shape[1]
    f_lo, f_hi = 3 * ATT_W, 3 * ATT_W + N_HEADS
    scale = HEAD_DIM ** -0.5
    w_main = jnp.concatenate([w_in[:, :, :ATT_W] * scale, w_in[:, :, ATT_W:f_lo], w_in[:, :, f_hi:]], axis=2)
    weights = {
        'w_main': w_main.astype(BF16),
        'w_f': jnp.pad(w_in[:, :, f_lo:f_hi], ((0, 0), (0, 0), (0, LANES - N_HEADS))).astype(BF16),
        'b_f': jnp.pad(b_f, ((0, 0), (0, LANES - N_HEADS))).reshape(depth, 1, LANES),
        'g_mix_pre': g_mix_pre.reshape(depth, 1, d),
        'g_mix_post': g_mix_post.reshape(depth, 1, d),
        'g_ffn_pre': g_ffn_pre.reshape(depth, 1, d),
        'g_ffn_post': g_ffn_post.reshape(depth, 1, d),
        'lam_re': lam_re,
        's5_mats': [_s5_matrices(lam_re[n], lam_im[n], log_dt[n], b_re[n], b_im[n], c_re[n], c_im[n], d_skip[n])
                    for n in range(depth)],
        'w_att_proj': w_att_proj.astype(BF16),
        'w_glu_v': w_glu_v.astype(BF16),
        'w_glu_g': w_glu_g.astype(BF16),
        'w_out': w_out.astype(BF16),
        'w_up': w_up.astype(BF16),
        'conv_w': jnp.pad(conv_w, ((0, 0), (0, SUBLANES - CONV_W), (0, 0))),
        'w_down': w_down.astype(BF16),
    }
    assert s5_w == lam_re.shape[1] * S5_GROUP
    sb, sp = cache_k.shape[1], cache_k.shape[2]
    caches = {
        'k': cache_k.transpose(0, 1, 3, 4, 2).reshape(depth, sb, ATT_W, sp),
        'v': cache_v.transpose(0, 1, 3, 4, 2).reshape(depth, sb, ATT_W, sp),
        'logf': cache_logf, 'ssm_re': state_ssm_re, 'ssm_im': state_ssm_im, 'conv': state_conv,
    }
    y_prompt, p_st = _trunk(x_prompt, weights, None)
    y_sample, s_st = _trunk(x_sample, weights, caches)
    return (y_prompt, y_sample, *p_st, *s_st)
```
